```python
import jax, jax.numpy as jnp
from jax import lax
import numpy as np

D_MODEL = 1024
BATCH = 16
SEQ = 256
DEPTH = 1
DEC_BATCH = 8
DEC_SEQ = 4096
PAST_LEN = 512

GRID_W = 64
RWKV_HEADS = 8
RWKV_HEAD_DIM = 64
RWKV_WIDTH = RWKV_HEADS * RWKV_HEAD_DIM
DECAY_LORA = 64
ICLR_LORA = 64
GATE_LORA = 128
N_DIRS = 2
FOURIER_GROUPS = 4
FOURIER_GROUP_DIM = 128
FOURIER_WIDTH = FOURIER_GROUPS * FOURIER_GROUP_DIM
N_BRANCHES = 2
D_FF = 2816
N_MOD = 9
NORM_EPS = 1e-6
GN_EPS = 64e-5

OFF_WD = 3 * RWKV_WIDTH
OFF_AD = OFF_WD + N_DIRS * DECAY_LORA
OFF_GD = OFF_AD + N_DIRS * ICLR_LORA
OFF_F = OFF_GD + GATE_LORA
OFF_GATE = OFF_F + FOURIER_WIDTH
PROJ_W = OFF_GATE + N_BRANCHES * D_MODEL

kernel_name = "hybrid_rwkv7_fnet_macaron_diffusion_step"


def rmsnorm(x, w):
    xf = x.astype(jnp.float32)
    y = xf * lax.rsqrt(jnp.mean(xf * xf, axis=-1, keepdims=True) + NORM_EPS)
    return (y * w.astype(jnp.float32)).astype(x.dtype)


def swiglu(h, w13, w2):
    a, b = jnp.split(h @ w13, 2, axis=-1)
    return (jax.nn.silu(a) * b) @ w2


def short_conv(x, k):
    xp = jnp.pad(x, ((0, 0), (1, 1), (0, 0)))
    return xp[:, :-2] * k[0] + xp[:, 1:-1] * k[1] + xp[:, 2:] * k[2]


def heads(t):
    return t.reshape(t.shape[:-1] + (RWKV_HEADS, RWKV_HEAD_DIM))


def wkv_scan(r, w, k, v, kk, a, s0, reverse):
    xs = tuple(jnp.moveaxis(t, 1, 0) for t in (r, w, k, v, kk, a))

    def step(S, inp):
        r_t, w_t, k_t, v_t, kk_t, a_t = inp
        sa = jnp.einsum('bhij,bhj->bhi', S, kk_t)
        S = (S * w_t[:, :, None, :]
             - sa[..., None] * (kk_t * a_t)[:, :, None, :]
             + v_t[..., None] * k_t[:, :, None, :])
        o = jnp.einsum('bhij,bhj->bhi', S, r_t)
        return S, o

    s_final, o = lax.scan(step, s0, xs, reverse=reverse)
    return jnp.moveaxis(o, 0, 1), s_final


def fourier_mix(f, grid):
    B, T, _ = f.shape
    ff = f.astype(jnp.float32).reshape(B, T, FOURIER_GROUPS, FOURIER_GROUP_DIM)
    if grid:
        rows = T // GRID_W
        ff = ff.reshape(B, rows, GRID_W, FOURIER_GROUPS, FOURIER_GROUP_DIM)
        out = jnp.fft.fftn(ff, axes=(1, 2, 4), norm="ortho").real
    else:
        out = jnp.fft.fftn(ff, axes=(1, 3), norm="ortho").real
    return out.reshape(B, T, FOURIER_WIDTH).astype(f.dtype)


def token_mixer(h, s_fwd0, s_bwd0, grid, w_in, gate_b, conv_rkv, w0, w_up, a0, a_up, g_up,
                k_k, k_a, r_k, lnx_w, lnx_b, w_o_rwkv, w_fourier, w_out):
    B, T, _ = h.shape
    f32 = jnp.float32
    u = h @ w_in
    rkv = short_conv(u[..., :OFF_WD], conv_rkv).astype(f32)
    r, k, v = jnp.split(rkv, 3, axis=-1)
    wd = u[..., OFF_WD:OFF_AD].astype(f32).reshape(B, T, N_DIRS, DECAY_LORA)
    ad = u[..., OFF_AD:OFF_GD].astype(f32).reshape(B, T, N_DIRS, ICLR_LORA)
    gd = u[..., OFF_GD:OFF_F].astype(f32)
    f = u[..., OFF_F:OFF_GATE]
    gates = jax.nn.sigmoid(u[..., OFF_GATE:].astype(f32).reshape(B, T, N_BRANCHES, D_MODEL)
                           + gate_b.astype(f32))
    w_log = -jax.nn.softplus(-(w0.astype(f32) + jnp.einsum('btdr,drc->btdc', jnp.tanh(wd), w_up.astype(f32)))) - 0.5
    decay = heads(jnp.exp(-jnp.exp(w_log)))
    a = jax.nn.sigmoid(a0.astype(f32) + jnp.einsum('btdr,drc->btdc', ad, a_up.astype(f32)))
    g = jax.nn.sigmoid(gd) @ g_up.astype(f32)
    kk = heads(k * k_k.astype(f32))
    kk = kk / jnp.maximum(jnp.sqrt(jnp.sum(kk * kk, axis=-1, keepdims=True)), 1e-12)
    k_dir = heads(k[:, :, None] * (1.0 + (a - 1.0) * k_a.astype(f32)))
    a_h = heads(a)
    r_h, v_h = heads(r), heads(v)
    o_f, s_f = wkv_scan(r_h, decay[:, :, 0], k_dir[:, :, 0], v_h, kk, a_h[:, :, 0],
                        s_fwd0.astype(f32), False)
    o_b, s_b = wkv_scan(r_h, decay[:, :, 1], k_dir[:, :, 1], v_h, kk, a_h[:, :, 1],
                        s_bwd0.astype(f32), True)
    o = o_f + o_b
    mu = jnp.mean(o, axis=-1, keepdims=True)
    var = jnp.mean(jnp.square(o - mu), axis=-1, keepdims=True)
    o = (o - mu) * lax.rsqrt(var + GN_EPS) * heads(lnx_w.astype(f32)) + heads(lnx_b.astype(f32))
    bonus = jnp.sum(jnp.sum(r_h[:, :, None] * k_dir * r_k.astype(f32), axis=-1, keepdims=True)
                    * v_h[:, :, None], axis=2)
    o = (o + bonus).reshape(B, T, RWKV_WIDTH) * g
    y_rwkv = o.astype(h.dtype) @ w_o_rwkv
    y_four = fourier_mix(f, grid) @ w_fourier
    merged = (gates[:, :, 0] * y_rwkv + gates[:, :, 1] * y_four).astype(h.dtype)
    return merged @ w_out, s_f, s_b


def trunk_layer(x, mod, s_fwd0, s_bwd0, grid, norm_w, ffn1_w13, ffn1_w2, ffn2_w13, ffn2_w2, mixer_params):
    m = mod[:, None].astype(x.dtype)
    h = rmsnorm(x, norm_w[0]) * (1.0 + m[:, :, 1]) + m[:, :, 0]
    x = x + 0.5 * m[:, :, 2] * swiglu(h, ffn1_w13, ffn1_w2)
    h = rmsnorm(x, norm_w[1]) * (1.0 + m[:, :, 4]) + m[:, :, 3]
    y, s_f, s_b = token_mixer(h, s_fwd0, s_bwd0, grid, *mixer_params)
    x = x + m[:, :, 5] * y
    h = rmsnorm(x, norm_w[2]) * (1.0 + m[:, :, 7]) + m[:, :, 6]
    x = x + 0.5 * m[:, :, 8] * swiglu(h, ffn2_w13, ffn2_w2)
    return x, s_f, s_b


def setup_inputs(seed: int = 0) -> dict:
    key = jax.random.key(seed)
    ks = iter(jax.random.split(key, 40))
    nrm = lambda shape, s: jax.random.normal(next(ks), shape, jnp.float32) * s
    L, D = DEPTH, D_MODEL
    st_shape = (DEC_BATCH, DEPTH, RWKV_HEADS, RWKV_HEAD_DIM, RWKV_HEAD_DIM)
    conv = nrm((L, 3, 3 * RWKV_WIDTH), 0.2)
    conv = conv.at[:, 1].add(1.0)
    return {
        "x_prompt": nrm((BATCH, SEQ, D), 1.0),
        "x_sample": nrm((DEC_BATCH, DEC_SEQ, D), 1.0),
        "state_rwkv_fwd": nrm(st_shape, 0.5),
        "state_rwkv_bwd": nrm(st_shape, 0.5),
        "c": nrm((DEC_BATCH, D), 1.0),
        "c_ctx": nrm((D,), 1.0),
        "ada_w": nrm((L, D, N_MOD * D), 0.5 * D ** -0.5),
        "ada_b": nrm((L, N_MOD * D), 0.01),
        "norm_w": 1.0 + nrm((L, 3, D), 0.02),
        "ffn1_w13": nrm((L, D, 2 * D_FF), D ** -0.5),
        "ffn1_w2": nrm((L, D_FF, D), D_FF ** -0.5),
        "ffn2_w13": nrm((L, D, 2 * D_FF), D ** -0.5),
        "ffn2_w2": nrm((L, D_FF, D), D_FF ** -0.5),
        "w_in": nrm((L, D, PROJ_W), D ** -0.5),
        "gate_b": nrm((L, N_BRANCHES, D), 0.1),
        "conv_rkv": conv,
        "w0": nrm((L, N_DIRS, RWKV_WIDTH), 2.0) - 2.0,
        "w_up": nrm((L, N_DIRS, DECAY_LORA, RWKV_WIDTH), 0.5 * DECAY_LORA ** -0.5),
        "a0": nrm((L, N_DIRS, RWKV_WIDTH), 0.5),
        "a_up": nrm((L, N_DIRS, ICLR_LORA, RWKV_WIDTH), 0.5 * ICLR_LORA ** -0.5),
        "g_up": nrm((L, GATE_LORA, RWKV_WIDTH), GATE_LORA ** -0.5),
        "k_k": 0.85 + nrm((L, RWKV_WIDTH), 0.1),
        "k_a": 1.0 + nrm((L, RWKV_WIDTH), 0.1),
        "r_k": nrm((L, RWKV_HEADS, RWKV_HEAD_DIM), 0.1),
        "lnx_w": 1.0 + nrm((L, RWKV_WIDTH), 0.02),
        "lnx_b": nrm((L, RWKV_WIDTH), 0.02),
        "w_o_rwkv": nrm((L, RWKV_WIDTH, D), RWKV_WIDTH ** -0.5),
        "w_fourier": nrm((L, FOURIER_WIDTH, D), FOURIER_WIDTH ** -0.5),
        "w_out": nrm((L, D, D), D ** -0.5),
        "final_norm": 1.0 + nrm((D,), 0.02),
    }


def reference(x_prompt, x_sample, state_rwkv_fwd, state_rwkv_bwd, c, c_ctx, ada_w, ada_b, norm_w,
              ffn1_w13, ffn1_w2, ffn2_w13, ffn2_w2, w_in, gate_b, conv_rkv, w0, w_up, a0, a_up, g_up,
              k_k, k_a, r_k, lnx_w, lnx_b, w_o_rwkv, w_fourier, w_out, final_norm):
    x_p, x_s = x_prompt, x_sample
    b_p, b_s = x_prompt.shape[0], x_sample.shape[0]
    zero_state = jnp.zeros((b_p, RWKV_HEADS, RWKV_HEAD_DIM, RWKV_HEAD_DIM), jnp.float32)
    new_f, new_b = [], []
    for l in range(DEPTH):
        mixer_params = (w_in[l], gate_b[l], conv_rkv[l], w0[l], w_up[l], a0[l], a_up[l], g_up[l],
                        k_k[l], k_a[l], r_k[l], lnx_w[l], lnx_b[l], w_o_rwkv[l], w_fourier[l], w_out[l])
        ffn = (norm_w[l], ffn1_w13[l], ffn1_w2[l], ffn2_w13[l], ffn2_w2[l])
        mod_ctx = (jax.nn.silu(c_ctx) @ ada_w[l] + ada_b[l]).reshape(1, N_MOD, D_MODEL)
        x_p, s_f, s_b = trunk_layer(x_p, mod_ctx, zero_state, zero_state, False, *ffn, mixer_params)
        new_f.append(s_f)
        new_b.append(s_b)
        mod_lat = (jax.nn.silu(c) @ ada_w[l] + ada_b[l]).reshape(b_s, N_MOD, D_MODEL)
        x_s, _, _ = trunk_layer(x_s, mod_lat, state_rwkv_fwd[:, l], state_rwkv_bwd[:, l], True,
                                *ffn, mixer_params)
    y_prompt = rmsnorm(x_p, final_norm)
    y_sample = rmsnorm(x_s, final_norm)
    new_state_fwd = jnp.stack(new_f, axis=1)
    new_state_bwd = jnp.stack(new_b, axis=1)
    return (y_prompt, y_sample, new_state_fwd, new_state_bwd)
```

```python
import functools

import numpy as np
import jax
import jax.numpy as jnp
from jax import lax
from jax.experimental import pallas as pl
from jax.experimental.pallas import tpu as pltpu

D_MODEL = 1024
RWKV_HEADS = 8
HEAD_DIM = 64
RWKV_WIDTH = RWKV_HEADS * HEAD_DIM
DECAY_LORA = 64
ICLR_LORA = 64
GATE_LORA = 128
N_DIRS = 2
FOURIER_GROUPS = 4
FOURIER_GROUP_DIM = 128
FOURIER_WIDTH = FOURIER_GROUPS * FOURIER_GROUP_DIM
GRID_W = 64
N_BRANCHES = 2
D_FF = 2816
N_MOD = 9
NORM_EPS = 1e-6
GN_EPS = 64e-5

OFF_WD = 3 * RWKV_WIDTH
LORA_W = N_DIRS * DECAY_LORA + N_DIRS * ICLR_LORA + GATE_LORA
OFF_F = OFF_WD + LORA_W
OFF_GATE = OFF_F + FOURIER_WIDTH
PROJ_W = OFF_GATE + N_BRANCHES * D_MODEL

CHUNK = 64
SUBLANES = 8
FF_CHUNK = 1408
VMEM_LIMIT = 56 * 1024 * 1024

F32 = jnp.float32
BF16 = jnp.bfloat16


def _bf(x):
    return x.astype(BF16)


def _dot(a, b):
    return jnp.dot(a, b, preferred_element_type=F32)


def _dot_nt(a, b):
    return lax.dot_general(a, b, (((1,), (1,)), ((), ())), preferred_element_type=F32)


def _dot_tn(a, b):
    return lax.dot_general(a, b, (((0,), (0,)), ((), ())), preferred_element_type=F32)


def _split(x, parts):
    out = []
    rem = x
    for _ in range(parts):
        p = rem.astype(BF16)
        out.append(p)
        rem = rem - p.astype(F32)
    return out


def _sigmoid(x):
    return 1.0 / (1.0 + jnp.exp(-x))


def _rms(x, w):
    ms = jnp.mean(x * x, axis=-1, keepdims=True)
    return x * lax.rsqrt(ms + NORM_EPS) * w


def _const_spec(shape):
    nd = len(shape)
    return pl.BlockSpec(shape, lambda *_: (0,) * nd, pipeline_mode=pl.Buffered(1))


def _params(sem):
    return pltpu.CompilerParams(dimension_semantics=sem, vmem_limit_bytes=VMEM_LIMIT)


def _mod_kernel(c_ref, w_ref, b_ref, o_ref):
    c = c_ref[...]
    s = c * _sigmoid(c)
    o_ref[...] = _dot(_bf(s), _bf(w_ref[...])) + b_ref[...]


def _modulation(cc, ada_w, ada_b):
    rows, d = cc.shape
    n = ada_w.shape[1]
    tn = n // 8
    return pl.pallas_call(
        _mod_kernel,
        grid=(n // tn,),
        in_specs=[pl.BlockSpec((rows, d), lambda j: (0, 0)),
                  pl.BlockSpec((d, tn), lambda j: (0, j)),
                  pl.BlockSpec((1, tn), lambda j: (0, j))],
        out_specs=pl.BlockSpec((rows, tn), lambda j: (0, j)),
        out_shape=jax.ShapeDtypeStruct((rows, n), F32),
        compiler_params=_params(("arbitrary",)),
        name="adaln_mod",
    )(cc, ada_w, ada_b.reshape(1, n))


def _ffn_kernel(x_ref, mod_ref, nw_ref, w13_ref, w2_ref, fn_ref, o_ref, *, mod_off, norm_row, final):
    x = x_ref[0]
    m = mod_ref[0]
    shift = m[mod_off:mod_off + 1]
    scale = m[mod_off + 1:mod_off + 2]
    gate = m[mod_off + 2:mod_off + 3]
    h = _rms(x, nw_ref[norm_row:norm_row + 1]) * (1.0 + scale) + shift
    hb = _bf(h)
    acc = jnp.zeros(x.shape, F32)
    for j in range(D_FF // FF_CHUNK):
        a = _dot(hb, w13_ref[:, j * FF_CHUNK:(j + 1) * FF_CHUNK])
        b = _dot(hb, w13_ref[:, D_FF + j * FF_CHUNK:D_FF + (j + 1) * FF_CHUNK])
        g = (a * _sigmoid(a)) * b
        acc = acc + _dot(_bf(g), w2_ref[j * FF_CHUNK:(j + 1) * FF_CHUNK, :])
    y = x + 0.5 * gate * acc
    if final:
        y = _rms(y, fn_ref[...])
    o_ref[0] = y


def _ffn(x, mod, per_batch_mod, norm_w, w13, w2, final_norm, *, mod_off, norm_row, final, tm):
    b, t, d = x.shape
    mod_map = (lambda bi, i: (bi + 1, 0, 0)) if per_batch_mod else (lambda bi, i: (0, 0, 0))
    kern = functools.partial(_ffn_kernel, mod_off=mod_off, norm_row=norm_row, final=final)
    return pl.pallas_call(
        kern,
        grid=(b, t // tm),
        in_specs=[pl.BlockSpec((1, tm, d), lambda bi, i: (bi, i, 0)),
                  pl.BlockSpec((1, N_MOD, d), mod_map),
                  _const_spec(norm_w.shape),
                  _const_spec(w13.shape),
                  _const_spec(w2.shape),
                  _const_spec((1, d))],
        out_specs=pl.BlockSpec((1, tm, d), lambda bi, i: (bi, i, 0)),
        out_shape=jax.ShapeDtypeStruct(x.shape, F32),
        compiler_params=_params(("arbitrary", "arbitrary")),
        name="ffn_final" if final else "ffn",
    )(x, mod, norm_w, w13, w2, final_norm.reshape(1, d))


def _inproj_kernel(x_ref, mod_ref, nw_ref, w_ref, rkv_ref, lora_ref, f_ref, gate_ref):
    x = x_ref[0]
    m = mod_ref[0]
    h = _rms(x, nw_ref[1:2]) * (1.0 + m[4:5]) + m[3:4]
    hb = _bf(h)
    rkv_ref[0] = _dot(hb, w_ref[:, :OFF_WD])
    lora_ref[0] = _dot(hb, w_ref[:, OFF_WD:OFF_F])
    f_ref[0] = _dot(hb, w_ref[:, OFF_F:OFF_GATE])
    gate_ref[0] = _dot(hb, w_ref[:, OFF_GATE:])


def _inproj(x, mod, per_batch_mod, norm_w, w_in, *, tm):
    b, t, d = x.shape
    mod_map = (lambda bi, i: (bi + 1, 0, 0)) if per_batch_mod else (lambda bi, i: (0, 0, 0))
    widths = (OFF_WD, LORA_W, FOURIER_WIDTH, N_BRANCHES * D_MODEL)
    return pl.pallas_call(
        _inproj_kernel,
        grid=(b, t // tm),
        in_specs=[pl.BlockSpec((1, tm, d), lambda bi, i: (bi, i, 0)),
                  pl.BlockSpec((1, N_MOD, d), mod_map),
                  _const_spec(norm_w.shape),
                  _const_spec(w_in.shape)],
        out_specs=[pl.BlockSpec((1, tm, w), lambda bi, i: (bi, i, 0)) for w in widths],
        out_shape=[jax.ShapeDtypeStruct((b, t, w), F32) for w in widths],
        compiler_params=_params(("arbitrary", "arbitrary")),
        name="mixer_inproj",
    )(x, mod, norm_w, w_in)


def _tri_inverse(ab, same_blk):
    c = ab.shape[0]
    eye = (lax.broadcasted_iota(jnp.int32, (c, c), 0) == lax.broadcasted_iota(jnp.int32, (c, c), 1))
    eye_f = eye.astype(F32)
    n1 = jnp.where(same_blk[SUBLANES], ab, 0.0)
    n1b = _bf(n1)
    n2 = _dot(n1b, n1b)
    n2b = _bf(n2)
    n4 = _dot(n2b, n2b)
    t = _dot(_bf(eye_f - n1), _bf(eye_f + n2))
    t = _dot(_bf(t), _bf(eye_f + n4))
    k = SUBLANES
    while k < c:
        off = jnp.where(same_blk[2 * k] & jnp.logical_not(same_blk[k]), ab, 0.0)
        tb = _bf(t)
        t = t - _dot(tb, _bf(_dot(_bf(off), tb)))
        k *= 2
    return t


def _scan_chunk(ci, n_chunks, reverse, d, rkv_ref, prev_ref, next_ref, lora_ref, prm, h_ref, o_ref,
                bonus_ref):
    c = CHUNK
    rw = RWKV_WIDTH
    u = rkv_ref[0]
    row = lax.broadcasted_iota(jnp.int32, (c, 1), 0)
    prev_row = jnp.where(ci > 0, prev_ref[0][SUBLANES - 1:SUBLANES, :], 0.0)
    next_row = jnp.where(ci < n_chunks - 1, next_ref[0][0:1, :], 0.0)
    u_prev = jnp.where(row == 0, prev_row, pltpu.roll(u, 1, 0))
    u_next = jnp.where(row == c - 1, next_row, pltpu.roll(u, c - 1, 0))
    cw = prm["conv"][...]
    rkv = u_prev * cw[0:1] + u * cw[1:2] + u_next * cw[2:3]
    r = rkv[:, :rw]
    k = rkv[:, rw:2 * rw]
    v = rkv[:, 2 * rw:]

    lo = lora_ref[0]
    wd = _bf(jnp.tanh(lo[:, :N_DIRS * DECAY_LORA]))
    ad = _bf(lo[:, N_DIRS * DECAY_LORA:N_DIRS * (DECAY_LORA + ICLR_LORA)])
    dsl = slice(d * rw, (d + 1) * rw)
    zw = prm["w0"][d:d + 1] + _dot(wd, prm["wup"][:, dsl])
    w_log = jnp.minimum(zw, 0.0) - jnp.log(1.0 + jnp.exp(-jnp.abs(zw))) - 0.5
    lw = -jnp.exp(w_log)
    a = _sigmoid(prm["a0"][d:d + 1] + _dot(ad, prm["aup"][:, dsl]))
    k_a = prm["k_a"][...]
    eseg = prm["eseg"][...]
    kk = k * prm["k_k"][...]
    ss = sum(_dot(p, eseg) for p in _split(kk * kk, 2))
    kk = kk / jnp.maximum(jnp.sqrt(ss), 1e-12)
    kd = k * (1.0 + (a - 1.0) * k_a)
    b = kk * a

    if bonus_ref is not None:
        osl = slice((1 - d) * rw, (2 - d) * rw)
        a_o = _sigmoid(prm["a0"][1 - d:2 - d] + _dot(ad, prm["aup"][:, osl]))
        kd_o = k * (1.0 + (a_o - 1.0) * k_a)
        rk = sum(_dot(p, eseg) for p in _split(r * prm["r_k"][...] * (kd + kd_o), 2))
        bonus_ref[0] = rk * v

    ti = lax.broadcasted_iota(jnp.int32, (c, c), 0)
    si = lax.broadcasted_iota(jnp.int32, (c, c), 1)
    if reverse:
        incl_m = si >= ti
        strict_m = si > ti
    else:
        incl_m = si <= ti
        strict_m = si < ti
    tri = _bf(incl_m.astype(F32))
    incl = sum(_dot(tri, p) for p in _split(lw, 3))
    total = incl[0:1] if reverse else incl[c - 1:c]
    e_ninc = jnp.exp(-incl)
    e_tot = jnp.exp(total - incl)
    kkm = kk * jnp.exp(incl - lw)
    rp = r * jnp.exp(incl)
    km = kd * e_ninc
    bm = b * e_ninc
    khat = kd * e_tot
    bhat = b * e_tot
    p_end = jnp.exp(total)

    same_blk = {}
    kb = SUBLANES
    while kb <= c:
        same_blk[kb] = (ti // kb) == (si // kb)
        kb *= 2
    eye = ti == si

    for h in range(RWKV_HEADS):
        hs = slice(h * HEAD_DIM, (h + 1) * HEAD_DIM)
        kkm_h = _bf(kkm[:, hs])
        rp_h = rp[:, hs]
        bhat_h = _bf(bhat[:, hs])
        v_h = _bf(v[:, hs])
        x2 = jnp.concatenate([kkm_h, _bf(rp_h)], axis=0)
        ab2 = _dot_nt(x2, _bf(bm[:, hs]))
        ak2 = _dot_nt(x2, _bf(km[:, hs]))
        a_b = jnp.where(strict_m, ab2[:c], 0.0)
        a_rb = _bf(jnp.where(incl_m, ab2[c:], 0.0))
        a_k = _bf(jnp.where(strict_m, ak2[:c], 0.0))
        a_rk = _bf(jnp.where(incl_m, ak2[c:], 0.0))
        tinv = _bf(_tri_inverse(a_b, same_blk))
        akv = _dot(a_k, v_h)
        wt = _dot(tinv, kkm_h)
        ut = _dot(tinv, _bf(akv))
        wtb = _bf(wt)
        utb = _bf(ut)
        qt = rp_h - _dot(a_rb, wtb)
        ot = _dot(a_rk, v_h) - _dot(a_rb, utb)
        mc = _dot_tn(bhat_h, wtb)
        g = _dot_tn(_bf(khat[:, hs]), v_h) - _dot_tn(bhat_h, utb)
        hh = h_ref[0, h]
        hb = _bf(hh)
        o_ref[0, :, hs] = _dot(_bf(qt), hb) + ot
        p_col = jnp.sum(jnp.where(eye, jnp.broadcast_to(p_end[:, hs], (c, c)), 0.0), axis=1,
                        keepdims=True)
        h_ref[0, h] = p_col * hh - _dot(_bf(mc), hb) + g


def _scan_kernel(rkv_f, prev_f, next_f, lora_f, rkv_b, prev_b, next_b, lora_b,
                 conv_ref, w0_ref, wup_ref, a0_ref, aup_ref, kk_ref, ka_ref, rk_ref, eseg_ref,
                 hf0_ref, hb0_ref, of_ref, ob_ref, bonus_ref, hf_ref, hb_ref, *, n_chunks):
    i = pl.program_id(1)

    @pl.when(i == 0)
    def _():
        hf_ref[...] = hf0_ref[...]
        hb_ref[...] = hb0_ref[...]

    prm = dict(conv=conv_ref, w0=w0_ref, wup=wup_ref, a0=a0_ref, aup=aup_ref, k_k=kk_ref,
               k_a=ka_ref, r_k=rk_ref, eseg=eseg_ref)
    _scan_chunk(i, n_chunks, False, 0, rkv_f, prev_f, next_f, lora_f, prm, hf_ref, of_ref, bonus_ref)
    _scan_chunk(n_chunks - 1 - i, n_chunks, True, 1, rkv_b, prev_b, next_b, lora_b, prm, hb_ref,
                ob_ref, None)


def _scan(u_rkv, u_lora, hf0, hb0, conv, w0, wup_bd, a0, aup_bd, k_k, k_a, r_k, eseg):
    b, t, _ = u_rkv.shape
    n = t // CHUNK
    per8 = CHUNK // SUBLANES
    last8 = t // SUBLANES - 1
    rw = RWKV_WIDTH

    def fwd(bi, i):
        return (bi, i, 0)

    def fwd_prev(bi, i):
        return (bi, jnp.maximum(i * per8 - 1, 0), 0)

    def fwd_next(bi, i):
        return (bi, jnp.minimum((i + 1) * per8, last8), 0)

    def bwd(bi, i):
        return (bi, n - 1 - i, 0)

    def bwd_prev(bi, i):
        return (bi, jnp.maximum((n - 1 - i) * per8 - 1, 0), 0)

    def bwd_next(bi, i):
        return (bi, jnp.minimum((n - i) * per8, last8), 0)

    def seq_specs(main, prev, nxt):
        return [pl.BlockSpec((1, CHUNK, 3 * rw), main),
                pl.BlockSpec((1, SUBLANES, 3 * rw), prev),
                pl.BlockSpec((1, SUBLANES, 3 * rw), nxt),
                pl.BlockSpec((1, CHUNK, LORA_W), main)]

    state_spec = pl.BlockSpec((1, RWKV_HEADS, HEAD_DIM, HEAD_DIM), lambda bi, i: (bi, 0, 0, 0))
    consts = (conv, w0, wup_bd, a0, aup_bd, k_k, k_a, r_k, eseg)
    seq_shape = jax.ShapeDtypeStruct((b, t, rw), F32)
    return pl.pallas_call(
        functools.partial(_scan_kernel, n_chunks=n),
        grid=(b, n),
        in_specs=(seq_specs(fwd, fwd_prev, fwd_next) + seq_specs(bwd, bwd_prev, bwd_next)
                  + [_const_spec(x.shape) for x in consts] + [state_spec, state_spec]),
        out_specs=[pl.BlockSpec((1, CHUNK, rw), fwd), pl.BlockSpec((1, CHUNK, rw), bwd),
                   pl.BlockSpec((1, CHUNK, rw), fwd), state_spec, state_spec],
        out_shape=[seq_shape, seq_shape, seq_shape,
                   jax.ShapeDtypeStruct(hf0.shape, F32), jax.ShapeDtypeStruct(hb0.shape, F32)],
        compiler_params=_params(("arbitrary", "arbitrary")),
        name="rwkv7_scan",
    )(u_rkv, u_rkv, u_rkv, u_lora, u_rkv, u_rkv, u_rkv, u_lora, *consts, hf0, hb0)


def _dft_tables(n):
    ang = 2.0 * np.pi * (np.outer(np.arange(n), np.arange(n)) % n) / n
    return np.cos(ang), np.sin(ang)


def _table(x):
    return jnp.asarray(x, F32).astype(BF16)


def _four_ctx_kernel(f_ref, wc_ref, ct_ref, st_ref, o_ref):
    y = _bf(_dot(_bf(f_ref[0]), wc_ref[...]))
    w = FOURIER_WIDTH
    o_ref[0] = _dot(ct_ref[...], y[:, :w]) - _dot(st_ref[...], y[:, w:])


def _fourier_ctx(f):
    b, t, w = f.shape
    gd = FOURIER_GROUP_DIM
    cc, sc = _dft_tables(gd)
    wc = np.zeros((w, 2 * w), np.float32)
    for g in range(FOURIER_GROUPS):
        wc[g * gd:(g + 1) * gd, g * gd:(g + 1) * gd] = cc
        wc[g * gd:(g + 1) * gd, w + g * gd:w + (g + 1) * gd] = sc
    ct, st = _dft_tables(t)
    norm = 1.0 / np.sqrt(t * gd)
    consts = [_table(wc), _table(ct * norm), _table(st * norm)]
    return pl.pallas_call(
        _four_ctx_kernel,
        grid=(b,),
        in_specs=[pl.BlockSpec((1, t, w), lambda bi: (bi, 0, 0))] + [_const_spec(x.shape) for x in consts],
        out_specs=pl.BlockSpec((1, t, w), lambda bi: (bi, 0, 0)),
        out_shape=jax.ShapeDtypeStruct(f.shape, F32),
        compiler_params=_params(("arbitrary",)),
        name="fourier_ctx",
    )(f, *consts)


W_ROWS = 4


def _four_lat_kernel(f_ref, wc_ref, kcw_ref, ksw_ref, kcr_ref, ksr_ref, o_ref, z_ref):
    gd = FOURIER_GROUP_DIM
    rows = f_ref.shape[1]
    x = _bf(f_ref[0].reshape(rows * GRID_W, gd))
    y = _bf(_dot(x, wc_ref[...]))
    kcw = kcw_ref[...]
    ksw = ksw_ref[...]
    blk = W_ROWS * GRID_W
    for j in range(rows // W_ROWS):
        yy = y[j * blk:(j + 1) * blk]
        pc = _dot(kcw, yy)
        ps = _dot(ksw, yy)
        z_ref[j * W_ROWS:(j + 1) * W_ROWS, :, :gd] = (pc[:, :gd] - ps[:, gd:]).reshape(W_ROWS, GRID_W, gd)
        z_ref[j * W_ROWS:(j + 1) * W_ROWS, :, gd:] = (ps[:, :gd] + pc[:, gd:]).reshape(W_ROWS, GRID_W, gd)
    kcr = kcr_ref[...]
    ksr = ksr_ref[...]
    for j in range(GRID_W // SUBLANES):
        ws = slice(j * SUBLANES, (j + 1) * SUBLANES)
        zg = _bf(z_ref[:, ws, :].reshape(rows * SUBLANES, 2 * gd))
        re = _dot(kcr, zg)[:, :gd] - _dot(ksr, zg)[:, gd:]
        o_ref[0, :, ws, :] = re.reshape(rows, SUBLANES, gd)


def _fourier_lat(f):
    b, t, w = f.shape
    gd = FOURIER_GROUP_DIM
    rows = t // GRID_W
    cc, sc = _dft_tables(gd)
    cw, sw = _dft_tables(GRID_W)
    cr, sr = _dft_tables(rows)
    norm = 1.0 / np.sqrt(rows * GRID_W * gd)
    eye_w = np.eye(W_ROWS)
    eye_8 = np.eye(SUBLANES)
    consts = [_table(np.concatenate([cc, sc], axis=1)),
              _table(np.kron(eye_w, cw)), _table(np.kron(eye_w, sw)),
              _table(np.kron(cr, eye_8) * norm), _table(np.kron(sr, eye_8) * norm)]
    f4 = f.reshape(b, rows, GRID_W, w)
    blk = pl.BlockSpec((1, rows, GRID_W, gd), lambda bi, g: (bi, 0, 0, g))
    out = pl.pallas_call(
        _four_lat_kernel,
        grid=(b, FOURIER_GROUPS),
        in_specs=[blk] + [_const_spec(x.shape) for x in consts],
        out_specs=blk,
        out_shape=jax.ShapeDtypeStruct(f4.shape, F32),
        scratch_shapes=[pltpu.VMEM((rows, GRID_W, 2 * gd), F32)],
        compiler_params=_params(("arbitrary", "arbitrary")),
        name="fourier_lat",
    )(f4, *consts)
    return out.reshape(b, t, w)


def _mixout_kernel(x_ref, mod_ref, of_ref, ob_ref, bonus_ref, lora_ref, four_ref, gate_ref,
                   eseg_ref, lnw_ref, lnb_ref, gup_ref, gb_ref, wo_ref, wf_ref, wout_ref, o_ref):
    eseg = eseg_ref[...]
    inv_n = 1.0 / HEAD_DIM
    o = of_ref[0] + ob_ref[0]
    mu = sum(_dot(p, eseg) for p in _split(o, 2)) * inv_n
    dlt = o - mu
    var = sum(_dot(p, eseg) for p in _split(dlt * dlt, 2)) * inv_n
    on = dlt * lax.rsqrt(var + GN_EPS) * lnw_ref[...] + lnb_ref[...]
    gd = lora_ref[0][:, N_DIRS * (DECAY_LORA + ICLR_LORA):]
    g = _dot(_bf(_sigmoid(gd)), gup_ref[...])
    y_rwkv = _dot(_bf((on + bonus_ref[0]) * g), wo_ref[...])
    y_four = _dot(_bf(four_ref[0]), wf_ref[...])
    gp = gate_ref[0]
    gb = gb_ref[...]
    g0 = _sigmoid(gp[:, :D_MODEL] + gb[0:1])
    g1 = _sigmoid(gp[:, D_MODEL:] + gb[1:2])
    merged = g0 * y_rwkv + g1 * y_four
    y = _dot(_bf(merged), wout_ref[...])
    o_ref[0] = x_ref[0] + mod_ref[0][5:6] * y


def _mixout(x, mod, per_batch_mod, o_f, o_b, bonus, u_lora, four, gate_pre, consts, *, tm):
    b, t, d = x.shape
    mod_map = (lambda bi, i: (bi + 1, 0, 0)) if per_batch_mod else (lambda bi, i: (0, 0, 0))

    def seq(w):
        return pl.BlockSpec((1, tm, w), lambda bi, i: (bi, i, 0))

    return pl.pallas_call(
        _mixout_kernel,
        grid=(b, t // tm),
        in_specs=([seq(d), pl.BlockSpec((1, N_MOD, d), mod_map), seq(RWKV_WIDTH), seq(RWKV_WIDTH),
                   seq(RWKV_WIDTH), seq(LORA_W), seq(FOURIER_WIDTH), seq(N_BRANCHES * D_MODEL)]
                  + [_const_spec(c.shape) for c in consts]),
        out_specs=seq(d),
        out_shape=jax.ShapeDtypeStruct(x.shape, F32),
        compiler_params=_params(("arbitrary", "arbitrary")),
        name="mixer_out",
    )(x, mod, o_f, o_b, bonus, u_lora, four, gate_pre, *consts)


def _block_diag_lora(w_up):
    dirs, rank, width = w_up.shape
    out = jnp.zeros((dirs * rank, dirs * width), F32)
    for d in range(dirs):
        out = out.at[d * rank:(d + 1) * rank, d * width:(d + 1) * width].set(w_up[d])
    return _bf(out)


def kernel(x_prompt, x_sample, state_rwkv_fwd, state_rwkv_bwd, c, c_ctx, ada_w, ada_b, norm_w,
           ffn1_w13, ffn1_w2, ffn2_w13, ffn2_w2, w_in, gate_b, conv_rkv, w0, w_up, a0, a_up, g_up,
           k_k, k_a, r_k, lnx_w, lnx_b, w_o_rwkv, w_fourier, w_out, final_norm):
    depth = ada_w.shape[0]
    b_p = x_prompt.shape[0]
    b_s = x_sample.shape[0]
    rw = RWKV_WIDTH
    seg = np.arange(rw) // HEAD_DIM
    eseg = jnp.asarray(seg[:, None] == seg[None, :], BF16)
    cc = jnp.zeros((2 * SUBLANES, D_MODEL), F32).at[0].set(c_ctx).at[1:1 + b_s].set(c)
    zero_state = jnp.zeros((b_p, RWKV_HEADS, HEAD_DIM, HEAD_DIM), F32)

    x_p, x_s = x_prompt, x_sample
    new_f, new_b = [], []
    for l in range(depth):
        mod = _modulation(cc, ada_w[l], ada_b[l])[:1 + b_s].reshape(1 + b_s, N_MOD, D_MODEL)
        w13_1, w2_1 = _bf(ffn1_w13[l]), _bf(ffn1_w2[l])
        w13_2, w2_2 = _bf(ffn2_w13[l]), _bf(ffn2_w2[l])
        w_in_b = _bf(w_in[l])
        scan_consts = (conv_rkv[l], w0[l], _block_diag_lora(w_up[l]), a0[l], _block_diag_lora(a_up[l]),
                       k_k[l].reshape(1, rw), k_a[l].reshape(1, rw), r_k[l].reshape(1, rw), eseg)
        mix_consts = (eseg, lnx_w[l].reshape(1, rw), lnx_b[l].reshape(1, rw), _bf(g_up[l]), gate_b[l],
                      _bf(w_o_rwkv[l]), _bf(w_fourier[l]), _bf(w_out[l]))
        last = l == depth - 1

        def layer(x, per_batch_mod, hf0, hb0, grid, tm):
            x = _ffn(x, mod, per_batch_mod, norm_w[l], w13_1, w2_1, final_norm,
                     mod_off=0, norm_row=0, final=False, tm=tm)
            u_rkv, u_lora, f, gate_pre = _inproj(x, mod, per_batch_mod, norm_w[l], w_in_b, tm=tm)
            o_f, o_b, bonus, hf, hb = _scan(u_rkv, u_lora, hf0, hb0, *scan_consts)
            four = _fourier_lat(f) if grid else _fourier_ctx(f)
            x = _mixout(x, mod, per_batch_mod, o_f, o_b, bonus, u_lora, four, gate_pre, mix_consts, tm=tm)
            x = _ffn(x, mod, per_batch_mod, norm_w[l], w13_2, w2_2, final_norm,
                     mod_off=6, norm_row=2, final=last, tm=tm)
            return x, hf, hb

        x_p, hf, hb = layer(x_p, False, zero_state, zero_state, False, 256)
        new_f.append(jnp.swapaxes(hf, -1, -2))
        new_b.append(jnp.swapaxes(hb, -1, -2))
        hf0 = jnp.swapaxes(state_rwkv_fwd[:, l], -1, -2)
        hb0 = jnp.swapaxes(state_rwkv_bwd[:, l], -1, -2)
        x_s, _, _ = layer(x_s, True, hf0, hb0, True, 512)
    if depth == 0:
        raise ValueError("depth must be positive")
    return (x_p, x_s, jnp.stack(new_f, axis=1), jnp.stack(new_b, axis=1))
```

```python
import functools

import numpy as np
import jax
import jax.numpy as jnp
from jax import lax
from jax.experimental import pallas as pl
from jax.experimental.pallas import tpu as pltpu

D_MODEL = 1024
RWKV_HEADS = 8
HEAD_DIM = 64
RWKV_WIDTH = RWKV_HEADS * HEAD_DIM
DECAY_LORA = 64
ICLR_LORA = 64
GATE_LORA = 128
N_DIRS = 2
FOURIER_GROUPS = 4
FOURIER_GROUP_DIM = 128
FOURIER_WIDTH = FOURIER_GROUPS * FOURIER_GROUP_DIM
GRID_W = 64
N_BRANCHES = 2
D_FF = 2816
N_MOD = 9
NORM_EPS = 1e-6
GN_EPS = 64e-5

OFF_WD = 3 * RWKV_WIDTH
LORA_W = N_DIRS * DECAY_LORA + N_DIRS * ICLR_LORA + GATE_LORA
OFF_F = OFF_WD + LORA_W
OFF_GATE = OFF_F + FOURIER_WIDTH
PROJ_W = OFF_GATE + N_BRANCHES * D_MODEL

CHUNK = 64
SUBLANES = 8
FF_CHUNK = 1408
VMEM_LIMIT = 56 * 1024 * 1024

F32 = jnp.float32
BF16 = jnp.bfloat16


def _bf(x):
    return x.astype(BF16)


def _dot(a, b):
    return jnp.dot(a, b, preferred_element_type=F32)


def _dot_nt(a, b):
    return lax.dot_general(a, b, (((1,), (1,)), ((), ())), preferred_element_type=F32)


def _dot_tn(a, b):
    return lax.dot_general(a, b, (((0,), (0,)), ((), ())), preferred_element_type=F32)


def _split(x, parts):
    out = []
    rem = x
    for _ in range(parts):
        p = rem.astype(BF16)
        out.append(p)
        rem = rem - p.astype(F32)
    return out


def _sigmoid(x):
    return 1.0 / (1.0 + jnp.exp(-x))


def _rms(x, w):
    ms = jnp.mean(x * x, axis=-1, keepdims=True)
    return x * lax.rsqrt(ms + NORM_EPS) * w


def _const_spec(shape):
    nd = len(shape)
    return pl.BlockSpec(shape, lambda *_: (0,) * nd, pipeline_mode=pl.Buffered(1))


def _params(sem):
    return pltpu.CompilerParams(dimension_semantics=sem, vmem_limit_bytes=VMEM_LIMIT)


def _mod_kernel(c_ref, w_ref, b_ref, o_ref):
    c = c_ref[...]
    s = c * _sigmoid(c)
    o_ref[...] = _dot(_bf(s), _bf(w_ref[...])) + b_ref[...]


def _modulation(cc, ada_w, ada_b):
    rows, d = cc.shape
    n = ada_w.shape[1]
    tn = n // 8
    return pl.pallas_call(
        _mod_kernel,
        grid=(n // tn,),
        in_specs=[pl.BlockSpec((rows, d), lambda j: (0, 0)),
                  pl.BlockSpec((d, tn), lambda j: (0, j)),
                  pl.BlockSpec((1, tn), lambda j: (0, j))],
        out_specs=pl.BlockSpec((rows, tn), lambda j: (0, j)),
        out_shape=jax.ShapeDtypeStruct((rows, n), F32),
        compiler_params=_params(("arbitrary",)),
        name="adaln_mod",
    )(cc, ada_w, ada_b.reshape(1, n))


def _ffn_kernel(x_ref, mod_ref, nw_ref, w13_ref, w2_ref, fn_ref, o_ref, *, mod_off, norm_row, final):
    x = x_ref[0]
    m = mod_ref[0]
    shift = m[mod_off:mod_off + 1]
    scale = m[mod_off + 1:mod_off + 2]
    gate = m[mod_off + 2:mod_off + 3]
    h = _rms(x, nw_ref[norm_row:norm_row + 1]) * (1.0 + scale) + shift
    hb = _bf(h)
    acc = jnp.zeros(x.shape, F32)
    for j in range(D_FF // FF_CHUNK):
        a = _dot(hb, w13_ref[:, j * FF_CHUNK:(j + 1) * FF_CHUNK])
        b = _dot(hb, w13_ref[:, D_FF + j * FF_CHUNK:D_FF + (j + 1) * FF_CHUNK])
        g = (a * _sigmoid(a)) * b
        acc = acc + _dot(_bf(g), w2_ref[j * FF_CHUNK:(j + 1) * FF_CHUNK, :])
    y = x + 0.5 * gate * acc
    if final:
        y = _rms(y, fn_ref[...])
    o_ref[0] = y


def _ffn(x, mod, per_batch_mod, norm_w, w13, w2, final_norm, *, mod_off, norm_row, final, tm):
    b, t, d = x.shape
    mod_map = (lambda bi, i: (bi + 1, 0, 0)) if per_batch_mod else (lambda bi, i: (0, 0, 0))
    kern = functools.partial(_ffn_kernel, mod_off=mod_off, norm_row=norm_row, final=final)
    return pl.pallas_call(
        kern,
        grid=(b, t // tm),
        in_specs=[pl.BlockSpec((1, tm, d), lambda bi, i: (bi, i, 0)),
                  pl.BlockSpec((1, N_MOD, d), mod_map),
                  _const_spec(norm_w.shape),
                  _const_spec(w13.shape),
                  _const_spec(w2.shape),
                  _const_spec((1, d))],
        out_specs=pl.BlockSpec((1, tm, d), lambda bi, i: (bi, i, 0)),
        out_shape=jax.ShapeDtypeStruct(x.shape, F32),
        compiler_params=_params(("arbitrary", "arbitrary")),
        name="ffn_final" if final else "ffn",
    )(x, mod, norm_w, w13, w2, final_norm.reshape(1, d))


def _inproj_kernel(x_ref, mod_ref, nw_ref, w_ref, rkv_ref, lora_ref, f_ref, gate_ref):
    x = x_ref[0]
    m = mod_ref[0]
    h = _rms(x, nw_ref[1:2]) * (1.0 + m[4:5]) + m[3:4]
    hb = _bf(h)
    rkv_ref[0] = _dot(hb, w_ref[:, :OFF_WD])
    lora_ref[0] = _dot(hb, w_ref[:, OFF_WD:OFF_F])
    f_ref[0] = _dot(hb, w_ref[:, OFF_F:OFF_GATE])
    gate_ref[0] = _dot(hb, w_ref[:, OFF_GATE:])


def _inproj(x, mod, per_batch_mod, norm_w, w_in, *, tm):
    b, t, d = x.shape
    mod_map = (lambda bi, i: (bi + 1, 0, 0)) if per_batch_mod else (lambda bi, i: (0, 0, 0))
    widths = (OFF_WD, LORA_W, FOURIER_WIDTH, N_BRANCHES * D_MODEL)
    return pl.pallas_call(
        _inproj_kernel,
        grid=(b, t // tm),
        in_specs=[pl.BlockSpec((1, tm, d), lambda bi, i: (bi, i, 0)),
                  pl.BlockSpec((1, N_MOD, d), mod_map),
                  _const_spec(norm_w.shape),
                  _const_spec(w_in.shape)],
        out_specs=[pl.BlockSpec((1, tm, w), lambda bi, i: (bi, i, 0)) for w in widths],
        out_shape=[jax.ShapeDtypeStruct((b, t, w), F32) for w in widths],
        compiler_params=_params(("arbitrary", "arbitrary")),
        name="mixer_inproj",
    )(x, mod, norm_w, w_in)


def _bdot(a, b):
    return lax.dot_general(a, b, (((2,), (1,)), ((0,), (0,))), preferred_element_type=F32)


def _bdot_nt(a, b):
    return lax.dot_general(a, b, (((2,), (2,)), ((0,), (0,))), preferred_element_type=F32)


def _tri_inverse(ab, same_blk, eye_f):
    c = ab.shape[-1]
    n1 = jnp.where(same_blk[SUBLANES], ab, 0.0)
    n1b = _bf(n1)
    n2 = _bdot(n1b, n1b)
    n2b = _bf(n2)
    n4 = _bdot(n2b, n2b)
    t = _bdot(_bf(eye_f - n1), _bf(eye_f + n2))
    t = _bdot(_bf(t), _bf(eye_f + n4))
    k = SUBLANES
    while k < c:
        off = jnp.where(same_blk[2 * k] & jnp.logical_not(same_blk[k]), ab, 0.0)
        tb = _bf(t)
        t = t - _bdot(tb, _bf(_bdot(_bf(off), tb)))
        k *= 2
    return t


def _heads(x):
    return jnp.stack([x[:, h * HEAD_DIM:(h + 1) * HEAD_DIM] for h in range(RWKV_HEADS)], axis=0)


def _chunk_operands(ci, n_chunks, reverse, d, rkv_ref, prev_ref, next_ref, lora_ref, prm, bonus_ref):
    c = CHUNK
    rw = RWKV_WIDTH
    u = rkv_ref[0]
    row = lax.broadcasted_iota(jnp.int32, (c, 1), 0)
    prev_row = jnp.where(ci > 0, prev_ref[0][SUBLANES - 1:SUBLANES, :], 0.0)
    next_row = jnp.where(ci < n_chunks - 1, next_ref[0][0:1, :], 0.0)
    u_prev = jnp.where(row == 0, prev_row, pltpu.roll(u, 1, 0))
    u_next = jnp.where(row == c - 1, next_row, pltpu.roll(u, c - 1, 0))
    cw = prm["conv"][...]
    rkv = u_prev * cw[0:1] + u * cw[1:2] + u_next * cw[2:3]
    r = rkv[:, :rw]
    k = rkv[:, rw:2 * rw]
    v = rkv[:, 2 * rw:]

    lo = lora_ref[0]
    wd = _bf(jnp.tanh(lo[:, :N_DIRS * DECAY_LORA]))
    ad = _bf(lo[:, N_DIRS * DECAY_LORA:N_DIRS * (DECAY_LORA + ICLR_LORA)])
    dsl = slice(d * rw, (d + 1) * rw)
    zw = prm["w0"][d:d + 1] + _dot(wd, prm["wup"][:, dsl])
    w_log = jnp.minimum(zw, 0.0) - jnp.log(1.0 + jnp.exp(-jnp.abs(zw))) - 0.5
    lw = -jnp.exp(w_log)
    a = _sigmoid(prm["a0"][d:d + 1] + _dot(ad, prm["aup"][:, dsl]))
    k_a = prm["k_a"][...]
    eseg = prm["eseg"][...]
    kk = k * prm["k_k"][...]
    ss = sum(_dot(p, eseg) for p in _split(kk * kk, 2))
    kk = kk / jnp.maximum(jnp.sqrt(ss), 1e-12)
    kd = k * (1.0 + (a - 1.0) * k_a)
    b = kk * a

    if bonus_ref is not None:
        osl = slice((1 - d) * rw, (2 - d) * rw)
        a_o = _sigmoid(prm["a0"][1 - d:2 - d] + _dot(ad, prm["aup"][:, osl]))
        kd_o = k * (1.0 + (a_o - 1.0) * k_a)
        rk = sum(_dot(p, eseg) for p in _split(r * prm["r_k"][...] * (kd + kd_o), 2))
        bonus_ref[0] = rk * v

    ti = lax.broadcasted_iota(jnp.int32, (c, c), 0)
    si = lax.broadcasted_iota(jnp.int32, (c, c), 1)
    tri = _bf(((si >= ti) if reverse else (si <= ti)).astype(F32))
    incl = sum(_dot(tri, p) for p in _split(lw, 3))
    total = incl[0:1] if reverse else incl[c - 1:c]
    e_ninc = jnp.exp(-incl)
    e_tot = jnp.exp(total - incl)
    return dict(kkm=_heads(kk * jnp.exp(incl - lw)), rp=_heads(r * jnp.exp(incl)),
                km=_heads(kd * e_ninc), bm=_heads(b * e_ninc), khat=_heads(kd * e_tot),
                bhat=_heads(b * e_tot), v=_heads(v),
                p_end=_heads(jnp.exp(total)))


def _scan_kernel(rkv_f, prev_f, next_f, lora_f, rkv_b, prev_b, next_b, lora_b,
                 conv_ref, w0_ref, wup_ref, a0_ref, aup_ref, kk_ref, ka_ref, rk_ref, eseg_ref,
                 hf0_ref, hb0_ref, of_ref, ob_ref, bonus_ref, hf_ref, hb_ref, *, n_chunks):
    i = pl.program_id(1)
    c = CHUNK
    nh = RWKV_HEADS
    g2 = N_DIRS * nh

    @pl.when(i == 0)
    def _():
        hf_ref[...] = hf0_ref[...]
        hb_ref[...] = hb0_ref[...]

    prm = dict(conv=conv_ref, w0=w0_ref, wup=wup_ref, a0=a0_ref, aup=aup_ref, k_k=kk_ref,
               k_a=ka_ref, r_k=rk_ref, eseg=eseg_ref)
    opf = _chunk_operands(i, n_chunks, False, 0, rkv_f, prev_f, next_f, lora_f, prm, bonus_ref)
    opb = _chunk_operands(n_chunks - 1 - i, n_chunks, True, 1, rkv_b, prev_b, next_b, lora_b, prm, None)
    op = {key: jnp.concatenate([opf[key], opb[key]], axis=0) for key in opf}

    rev = lax.broadcasted_iota(jnp.int32, (g2, c, c), 0) >= nh
    ti = lax.broadcasted_iota(jnp.int32, (g2, c, c), 1)
    si = lax.broadcasted_iota(jnp.int32, (g2, c, c), 2)
    eye = ti == si
    age = jnp.where(rev, ti - si, si - ti)
    strict_m = age < 0
    incl_m = age <= 0
    same_blk = {}
    kb = SUBLANES
    while kb <= c:
        same_blk[kb] = (ti // kb) == (si // kb)
        kb *= 2

    kkm = _bf(op["kkm"])
    rp = op["rp"]
    vb = _bf(op["v"])
    x2 = jnp.concatenate([kkm, _bf(rp)], axis=1)
    ab2 = _bdot_nt(x2, _bf(op["bm"]))
    ak2 = _bdot_nt(x2, _bf(op["km"]))
    a_b = jnp.where(strict_m, ab2[:, :c], 0.0)
    a_rb = _bf(jnp.where(incl_m, ab2[:, c:], 0.0))
    a_k = _bf(jnp.where(strict_m, ak2[:, :c], 0.0))
    a_rk = _bf(jnp.where(incl_m, ak2[:, c:], 0.0))
    tinv = _bf(_tri_inverse(a_b, same_blk, eye.astype(F32)))
    akv = _bdot(a_k, vb)
    wtb = _bf(_bdot(tinv, kkm))
    utb = _bf(_bdot(tinv, _bf(akv)))
    qt = rp - _bdot(a_rb, wtb)
    ot = _bdot(a_rk, vb) - _bdot(a_rb, utb)
    bhat_t = _bf(jnp.swapaxes(op["bhat"], 1, 2))
    khat_t = _bf(jnp.swapaxes(op["khat"], 1, 2))
    mc = _bdot(bhat_t, wtb)
    g = _bdot(khat_t, vb) - _bdot(bhat_t, utb)
    hh = jnp.concatenate([hf_ref[0], hb_ref[0]], axis=0)
    hb16 = _bf(hh)
    o = _bdot(_bf(qt), hb16) + ot
    p_col = jnp.sum(jnp.where(eye, jnp.broadcast_to(op["p_end"], (g2, c, c)), 0.0), axis=2, keepdims=True)
    h_new = p_col * hh - _bdot(_bf(mc), hb16) + g
    hf_ref[0] = h_new[:nh]
    hb_ref[0] = h_new[nh:]
    for h in range(nh):
        hs = slice(h * HEAD_DIM, (h + 1) * HEAD_DIM)
        of_ref[0, :, hs] = o[h]
        ob_ref[0, :, hs] = o[nh + h]


def _scan(u_rkv, u_lora, hf0, hb0, conv, w0, wup_bd, a0, aup_bd, k_k, k_a, r_k, eseg):
    b, t, _ = u_rkv.shape
    n = t // CHUNK
    per8 = CHUNK // SUBLANES
    last8 = t // SUBLANES - 1
    rw = RWKV_WIDTH

    def fwd(bi, i):
        return (bi, i, 0)

    def fwd_prev(bi, i):
        return (bi, jnp.maximum(i * per8 - 1, 0), 0)

    def fwd_next(bi, i):
        return (bi, jnp.minimum((i + 1) * per8, last8), 0)

    def bwd(bi, i):
        return (bi, n - 1 - i, 0)

    def bwd_prev(bi, i):
        return (bi, jnp.maximum((n - 1 - i) * per8 - 1, 0), 0)

    def bwd_next(bi, i):
        return (bi, jnp.minimum((n - i) * per8, last8), 0)

    def seq_specs(main, prev, nxt):
        return [pl.BlockSpec((1, CHUNK, 3 * rw), main),
                pl.BlockSpec((1, SUBLANES, 3 * rw), prev),
                pl.BlockSpec((1, SUBLANES, 3 * rw), nxt),
                pl.BlockSpec((1, CHUNK, LORA_W), main)]

    state_spec = pl.BlockSpec((1, RWKV_HEADS, HEAD_DIM, HEAD_DIM), lambda bi, i: (bi, 0, 0, 0))
    consts = (conv, w0, wup_bd, a0, aup_bd, k_k, k_a, r_k, eseg)
    seq_shape = jax.ShapeDtypeStruct((b, t, rw), F32)
    return pl.pallas_call(
        functools.partial(_scan_kernel, n_chunks=n),
        grid=(b, n),
        in_specs=(seq_specs(fwd, fwd_prev, fwd_next) + seq_specs(bwd, bwd_prev, bwd_next)
                  + [_const_spec(x.shape) for x in consts] + [state_spec, state_spec]),
        out_specs=[pl.BlockSpec((1, CHUNK, rw), fwd), pl.BlockSpec((1, CHUNK, rw), bwd),
                   pl.BlockSpec((1, CHUNK, rw), fwd), state_spec, state_spec],
        out_shape=[seq_shape, seq_shape, seq_shape,
                   jax.ShapeDtypeStruct(hf0.shape, F32), jax.ShapeDtypeStruct(hb0.shape, F32)],
        compiler_params=_params(("arbitrary", "arbitrary")),
        name="rwkv7_scan",
    )(u_rkv, u_rkv, u_rkv, u_lora, u_rkv, u_rkv, u_rkv, u_lora, *consts, hf0, hb0)


def _dft_tables(n):
    ang = 2.0 * np.pi * (np.outer(np.arange(n), np.arange(n)) % n) / n
    return np.cos(ang), np.sin(ang)


def _table(x):
    return jnp.asarray(x, F32).astype(BF16)


def _four_ctx_kernel(f_ref, wc_ref, ct_ref, st_ref, o_ref):
    y = _bf(_dot(_bf(f_ref[0]), wc_ref[...]))
    w = FOURIER_WIDTH
    o_ref[0] = _dot(ct_ref[...], y[:, :w]) - _dot(st_ref[...], y[:, w:])


def _fourier_ctx(f):
    b, t, w = f.shape
    gd = FOURIER_GROUP_DIM
    cc, sc = _dft_tables(gd)
    wc = np.zeros((w, 2 * w), np.float32)
    for g in range(FOURIER_GROUPS):
        wc[g * gd:(g + 1) * gd, g * gd:(g + 1) * gd] = cc
        wc[g * gd:(g + 1) * gd, w + g * gd:w + (g + 1) * gd] = sc
    ct, st = _dft_tables(t)
    norm = 1.0 / np.sqrt(t * gd)
    consts = [_table(wc), _table(ct * norm), _table(st * norm)]
    return pl.pallas_call(
        _four_ctx_kernel,
        grid=(b,),
        in_specs=[pl.BlockSpec((1, t, w), lambda bi: (bi, 0, 0))] + [_const_spec(x.shape) for x in consts],
        out_specs=pl.BlockSpec((1, t, w), lambda bi: (bi, 0, 0)),
        out_shape=jax.ShapeDtypeStruct(f.shape, F32),
        compiler_params=_params(("arbitrary",)),
        name="fourier_ctx",
    )(f, *consts)


W_ROWS = 4


def _four_lat_kernel(f_ref, wc_ref, kcw_ref, ksw_ref, kcr_ref, ksr_ref, o_ref, z_ref):
    gd = FOURIER_GROUP_DIM
    rows = f_ref.shape[1]
    x = _bf(f_ref[0].reshape(rows * GRID_W, gd))
    y = _bf(_dot(x, wc_ref[...]))
    kcw = kcw_ref[...]
    ksw = ksw_ref[...]
    blk = W_ROWS * GRID_W
    for j in range(rows // W_ROWS):
        yy = y[j * blk:(j + 1) * blk]
        pc = _dot(kcw, yy)
        ps = _dot(ksw, yy)
        z_ref[j * W_ROWS:(j + 1) * W_ROWS, :, :gd] = (pc[:, :gd] - ps[:, gd:]).reshape(W_ROWS, GRID_W, gd)
        z_ref[j * W_ROWS:(j + 1) * W_ROWS, :, gd:] = (ps[:, :gd] + pc[:, gd:]).reshape(W_ROWS, GRID_W, gd)
    kcr = kcr_ref[...]
    ksr = ksr_ref[...]
    for j in range(GRID_W // SUBLANES):
        ws = slice(j * SUBLANES, (j + 1) * SUBLANES)
        zg = _bf(z_ref[:, ws, :].reshape(rows * SUBLANES, 2 * gd))
        re = _dot(kcr, zg)[:, :gd] - _dot(ksr, zg)[:, gd:]
        o_ref[0, :, ws, :] = re.reshape(rows, SUBLANES, gd)


def _fourier_lat(f):
    b, t, w = f.shape
    gd = FOURIER_GROUP_DIM
    rows = t // GRID_W
    cc, sc = _dft_tables(gd)
    cw, sw = _dft_tables(GRID_W)
    cr, sr = _dft_tables(rows)
    norm = 1.0 / np.sqrt(rows * GRID_W * gd)
    eye_w = np.eye(W_ROWS)
    eye_8 = np.eye(SUBLANES)
    consts = [_table(np.concatenate([cc, sc], axis=1)),
              _table(np.kron(eye_w, cw)), _table(np.kron(eye_w, sw)),
              _table(np.kron(cr, eye_8) * norm), _table(np.kron(sr, eye_8) * norm)]
    f4 = f.reshape(b, rows, GRID_W, w)
    blk = pl.BlockSpec((1, rows, GRID_W, gd), lambda bi, g: (bi, 0, 0, g))
    out = pl.pallas_call(
        _four_lat_kernel,
        grid=(b, FOURIER_GROUPS),
        in_specs=[blk] + [_const_spec(x.shape) for x in consts],
        out_specs=blk,
        out_shape=jax.ShapeDtypeStruct(f4.shape, F32),
        scratch_shapes=[pltpu.VMEM((rows, GRID_W, 2 * gd), F32)],
        compiler_params=_params(("arbitrary", "arbitrary")),
        name="fourier_lat",
    )(f4, *consts)
    return out.reshape(b, t, w)


def _mixout_kernel(x_ref, mod_ref, of_ref, ob_ref, bonus_ref, lora_ref, four_ref, gate_ref,
                   eseg_ref, lnw_ref, lnb_ref, gup_ref, gb_ref, wo_ref, wf_ref, wout_ref, o_ref):
    eseg = eseg_ref[...]
    inv_n = 1.0 / HEAD_DIM
    o = of_ref[0] + ob_ref[0]
    mu = sum(_dot(p, eseg) for p in _split(o, 2)) * inv_n
    dlt = o - mu
    var = sum(_dot(p, eseg) for p in _split(dlt * dlt, 2)) * inv_n
    on = dlt * lax.rsqrt(var + GN_EPS) * lnw_ref[...] + lnb_ref[...]
    gd = lora_ref[0][:, N_DIRS * (DECAY_LORA + ICLR_LORA):]
    g = _dot(_bf(_sigmoid(gd)), gup_ref[...])
    y_rwkv = _dot(_bf((on + bonus_ref[0]) * g), wo_ref[...])
    y_four = _dot(_bf(four_ref[0]), wf_ref[...])
    gp = gate_ref[0]
    gb = gb_ref[...]
    g0 = _sigmoid(gp[:, :D_MODEL] + gb[0:1])
    g1 = _sigmoid(gp[:, D_MODEL:] + gb[1:2])
    merged = g0 * y_rwkv + g1 * y_four
    y = _dot(_bf(merged), wout_ref[...])
    o_ref[0] = x_ref[0] + mod_ref[0][5:6] * y


def _mixout(x, mod, per_batch_mod, o_f, o_b, bonus, u_lora, four, gate_pre, consts, *, tm):
    b, t, d = x.shape
    mod_map = (lambda bi, i: (bi + 1, 0, 0)) if per_batch_mod else (lambda bi, i: (0, 0, 0))

    def seq(w):
        return pl.BlockSpec((1, tm, w), lambda bi, i: (bi, i, 0))

    return pl.pallas_call(
        _mixout_kernel,
        grid=(b, t // tm),
        in_specs=([seq(d), pl.BlockSpec((1, N_MOD, d), mod_map), seq(RWKV_WIDTH), seq(RWKV_WIDTH),
                   seq(RWKV_WIDTH), seq(LORA_W), seq(FOURIER_WIDTH), seq(N_BRANCHES * D_MODEL)]
                  + [_const_spec(c.shape) for c in consts]),
        out_specs=seq(d),
        out_shape=jax.ShapeDtypeStruct(x.shape, F32),
        compiler_params=_params(("arbitrary", "arbitrary")),
        name="mixer_out",
    )(x, mod, o_f, o_b, bonus, u_lora, four, gate_pre, *consts)


def _block_diag_lora(w_up):
    dirs, rank, width = w_up.shape
    out = jnp.zeros((dirs * rank, dirs * width), F32)
    for d in range(dirs):
        out = out.at[d * rank:(d + 1) * rank, d * width:(d + 1) * width].set(w_up[d])
    return _bf(out)


def kernel(x_prompt, x_sample, state_rwkv_fwd, state_rwkv_bwd, c, c_ctx, ada_w, ada_b, norm_w,
           ffn1_w13, ffn1_w2, ffn2_w13, ffn2_w2, w_in, gate_b, conv_rkv, w0, w_up, a0, a_up, g_up,
           k_k, k_a, r_k, lnx_w, lnx_b, w_o_rwkv, w_fourier, w_out, final_norm):
    depth = ada_w.shape[0]
    b_p = x_prompt.shape[0]
    b_s = x_sample.shape[0]
    rw = RWKV_WIDTH
    seg = np.arange(rw) // HEAD_DIM
    eseg = jnp.asarray(seg[:, None] == seg[None, :], BF16)
    cc = jnp.zeros((2 * SUBLANES, D_MODEL), F32).at[0].set(c_ctx).at[1:1 + b_s].set(c)
    zero_state = jnp.zeros((b_p, RWKV_HEADS, HEAD_DIM, HEAD_DIM), F32)

    x_p, x_s = x_prompt, x_sample
    new_f, new_b = [], []
    for l in range(depth):
        mod = _modulation(cc, ada_w[l], ada_b[l])[:1 + b_s].reshape(1 + b_s, N_MOD, D_MODEL)
        w13_1, w2_1 = _bf(ffn1_w13[l]), _bf(ffn1_w2[l])
        w13_2, w2_2 = _bf(ffn2_w13[l]), _bf(ffn2_w2[l])
        w_in_b = _bf(w_in[l])
        scan_consts = (conv_rkv[l], w0[l], _block_diag_lora(w_up[l]), a0[l], _block_diag_lora(a_up[l]),
                       k_k[l].reshape(1, rw), k_a[l].reshape(1, rw), r_k[l].reshape(1, rw), eseg)
        mix_consts = (eseg, lnx_w[l].reshape(1, rw), lnx_b[l].reshape(1, rw), _bf(g_up[l]), gate_b[l],
                      _bf(w_o_rwkv[l]), _bf(w_fourier[l]), _bf(w_out[l]))
        last = l == depth - 1

        def layer(x, per_batch_mod, hf0, hb0, grid, tm):
            x = _ffn(x, mod, per_batch_mod, norm_w[l], w13_1, w2_1, final_norm,
                     mod_off=0, norm_row=0, final=False, tm=tm)
            u_rkv, u_lora, f, gate_pre = _inproj(x, mod, per_batch_mod, norm_w[l], w_in_b, tm=tm)
            o_f, o_b, bonus, hf, hb = _scan(u_rkv, u_lora, hf0, hb0, *scan_consts)
            four = _fourier_lat(f) if grid else _fourier_ctx(f)
            x = _mixout(x, mod, per_batch_mod, o_f, o_b, bonus, u_lora, four, gate_pre, mix_consts, tm=tm)
            x = _ffn(x, mod, per_batch_mod, norm_w[l], w13_2, w2_2, final_norm,
                     mod_off=6, norm_row=2, final=last, tm=tm)
            return x, hf, hb

        x_p, hf, hb = layer(x_p, False, zero_state, zero_state, False, 256)
        new_f.append(jnp.swapaxes(hf, -1, -2))
        new_b.append(jnp.swapaxes(hb, -1, -2))
        hf0 = jnp.swapaxes(state_rwkv_fwd[:, l], -1, -2)
        hb0 = jnp.swapaxes(state_rwkv_bwd[:, l], -1, -2)
        x_s, _, _ = layer(x_s, True, hf0, hb0, True, 512)
    if depth == 0:
        raise ValueError("depth must be positive")
    return (x_p, x_s, jnp.stack(new_f, axis=1), jnp.stack(new_b, axis=1))
```

```python
import functools

import numpy as np
import jax
import jax.numpy as jnp
from jax import lax
from jax.experimental import pallas as pl
from jax.experimental.pallas import tpu as pltpu

D_MODEL = 1024
RWKV_HEADS = 8
HEAD_DIM = 64
RWKV_WIDTH = RWKV_HEADS * HEAD_DIM
DECAY_LORA = 64
ICLR_LORA = 64
GATE_LORA = 128
N_DIRS = 2
FOURIER_GROUPS = 4
FOURIER_GROUP_DIM = 128
FOURIER_WIDTH = FOURIER_GROUPS * FOURIER_GROUP_DIM
GRID_W = 64
N_BRANCHES = 2
D_FF = 2816
N_MOD = 9
NORM_EPS = 1e-6
GN_EPS = 64e-5

OFF_WD = 3 * RWKV_WIDTH
LORA_W = N_DIRS * DECAY_LORA + N_DIRS * ICLR_LORA + GATE_LORA
OFF_F = OFF_WD + LORA_W
OFF_GATE = OFF_F + FOURIER_WIDTH
PROJ_W = OFF_GATE + N_BRANCHES * D_MODEL

CHUNK = 64
PAIR = 2 * HEAD_DIM
N_PAIRS = RWKV_HEADS // 2
SUB_CHUNKS = 2
SUBLANES = 8
FF_CHUNK = 1408
VMEM_LIMIT = 56 * 1024 * 1024

F32 = jnp.float32
BF16 = jnp.bfloat16


def _bf(x):
    return x.astype(BF16)


def _dot(a, b):
    return jnp.dot(a, b, preferred_element_type=F32)


def _split(x, parts):
    out = []
    rem = x
    for _ in range(parts):
        p = rem.astype(BF16)
        out.append(p)
        rem = rem - p.astype(F32)
    return out


def _sigmoid(x):
    return 1.0 / (1.0 + jnp.exp(-x))


def _rms(x, w):
    ms = jnp.mean(x * x, axis=-1, keepdims=True)
    return x * lax.rsqrt(ms + NORM_EPS) * w


def _const_spec(shape):
    nd = len(shape)
    return pl.BlockSpec(shape, lambda *_: (0,) * nd, pipeline_mode=pl.Buffered(1))


def _params(sem):
    return pltpu.CompilerParams(dimension_semantics=sem, vmem_limit_bytes=VMEM_LIMIT)


def _mod_kernel(c_ref, w_ref, b_ref, o_ref):
    c = c_ref[...]
    s = c * _sigmoid(c)
    o_ref[...] = _dot(_bf(s), _bf(w_ref[...])) + b_ref[...]


def _modulation(cc, ada_w, ada_b):
    rows, d = cc.shape
    n = ada_w.shape[1]
    tn = n // 8
    return pl.pallas_call(
        _mod_kernel,
        grid=(n // tn,),
        in_specs=[pl.BlockSpec((rows, d), lambda j: (0, 0)),
                  pl.BlockSpec((d, tn), lambda j: (0, j)),
                  pl.BlockSpec((1, tn), lambda j: (0, j))],
        out_specs=pl.BlockSpec((rows, tn), lambda j: (0, j)),
        out_shape=jax.ShapeDtypeStruct((rows, n), F32),
        compiler_params=_params(("arbitrary",)),
        name="adaln_mod",
    )(cc, ada_w, ada_b.reshape(1, n))


def _ffn_kernel(x_ref, mod_ref, nw_ref, w13_ref, w2_ref, fn_ref, o_ref, *, mod_off, norm_row, final):
    x = x_ref[0]
    m = mod_ref[0]
    shift = m[mod_off:mod_off + 1]
    scale = m[mod_off + 1:mod_off + 2]
    gate = m[mod_off + 2:mod_off + 3]
    h = _rms(x, nw_ref[norm_row:norm_row + 1]) * (1.0 + scale) + shift
    hb = _bf(h)
    acc = jnp.zeros(x.shape, F32)
    for j in range(D_FF // FF_CHUNK):
        a = _dot(hb, w13_ref[:, j * FF_CHUNK:(j + 1) * FF_CHUNK])
        b = _dot(hb, w13_ref[:, D_FF + j * FF_CHUNK:D_FF + (j + 1) * FF_CHUNK])
        g = (a * _sigmoid(a)) * b
        acc = acc + _dot(_bf(g), w2_ref[j * FF_CHUNK:(j + 1) * FF_CHUNK, :])
    y = x + 0.5 * gate * acc
    if final:
        y = _rms(y, fn_ref[...])
    o_ref[0] = y


def _ffn(x, mod, per_batch_mod, norm_w, w13, w2, final_norm, *, mod_off, norm_row, final, tm):
    b, t, d = x.shape
    mod_map = (lambda bi, i: (bi + 1, 0, 0)) if per_batch_mod else (lambda bi, i: (0, 0, 0))
    kern = functools.partial(_ffn_kernel, mod_off=mod_off, norm_row=norm_row, final=final)
    return pl.pallas_call(
        kern,
        grid=(b, t // tm),
        in_specs=[pl.BlockSpec((1, tm, d), lambda bi, i: (bi, i, 0)),
                  pl.BlockSpec((1, N_MOD, d), mod_map),
                  _const_spec(norm_w.shape),
                  _const_spec(w13.shape),
                  _const_spec(w2.shape),
                  _const_spec((1, d))],
        out_specs=pl.BlockSpec((1, tm, d), lambda bi, i: (bi, i, 0)),
        out_shape=jax.ShapeDtypeStruct(x.shape, F32),
        compiler_params=_params(("arbitrary", "arbitrary")),
        name="ffn_final" if final else "ffn",
    )(x, mod, norm_w, w13, w2, final_norm.reshape(1, d))


def _inproj_kernel(x_ref, mod_ref, nw_ref, w_ref, rkv_ref, lora_ref, f_ref, gate_ref):
    x = x_ref[0]
    m = mod_ref[0]
    h = _rms(x, nw_ref[1:2]) * (1.0 + m[4:5]) + m[3:4]
    hb = _bf(h)
    rkv_ref[0] = _dot(hb, w_ref[:, :OFF_WD])
    lora_ref[0] = _dot(hb, w_ref[:, OFF_WD:OFF_F])
    f_ref[0] = _dot(hb, w_ref[:, OFF_F:OFF_GATE])
    gate_ref[0] = _dot(hb, w_ref[:, OFF_GATE:])


def _inproj(x, mod, per_batch_mod, norm_w, w_in, *, tm):
    b, t, d = x.shape
    mod_map = (lambda bi, i: (bi + 1, 0, 0)) if per_batch_mod else (lambda bi, i: (0, 0, 0))
    widths = (OFF_WD, LORA_W, FOURIER_WIDTH, N_BRANCHES * D_MODEL)
    return pl.pallas_call(
        _inproj_kernel,
        grid=(b, t // tm),
        in_specs=[pl.BlockSpec((1, tm, d), lambda bi, i: (bi, i, 0)),
                  pl.BlockSpec((1, N_MOD, d), mod_map),
                  _const_spec(norm_w.shape),
                  _const_spec(w_in.shape)],
        out_specs=[pl.BlockSpec((1, tm, w), lambda bi, i: (bi, i, 0)) for w in widths],
        out_shape=[jax.ShapeDtypeStruct((b, t, w), F32) for w in widths],
        compiler_params=_params(("arbitrary", "arbitrary")),
        name="mixer_inproj",
    )(x, mod, norm_w, w_in)


def _bdot(a, b):
    return lax.dot_general(a, b, (((2,), (1,)), ((0,), (0,))), preferred_element_type=F32)


def _bdot_nt(a, b):
    return lax.dot_general(a, b, (((2,), (2,)), ((0,), (0,))), preferred_element_type=F32)


def _tri_inverse(ab, same_blk, eye_f, bd):
    c = ab.shape[1]
    n1 = jnp.where(same_blk[SUBLANES], ab, 0.0)
    n2 = _bdot(_bf(n1), bd(n1))
    n4 = _bdot(_bf(n2), bd(n2))
    t = _bdot(_bf(eye_f - n1), bd(eye_f + n2))
    t = _bdot(_bf(t), bd(eye_f + n4))
    k = SUBLANES
    while k < c:
        off = jnp.where(same_blk[2 * k] & jnp.logical_not(same_blk[k]), ab, 0.0)
        t = t - _bdot(_bf(t), bd(_bdot(_bf(off), bd(t))))
        k *= 2
    return t


def _pairs(x):
    return jnp.stack([x[k * CHUNK:(k + 1) * CHUNK, j * PAIR:(j + 1) * PAIR]
                      for k in range(x.shape[0] // CHUNK) for j in range(N_PAIRS)], axis=0)


def _block_operands(bi, n_blocks, reverse, d, rkv_ref, prev_ref, next_ref, lora_ref, prm, bonus_ref):
    c = CHUNK
    rw = RWKV_WIDTH
    u = rkv_ref[0]
    rows = u.shape[0]
    row = lax.broadcasted_iota(jnp.int32, (rows, 1), 0)
    prev_row = jnp.where(bi > 0, prev_ref[0][SUBLANES - 1:SUBLANES, :], 0.0)
    next_row = jnp.where(bi < n_blocks - 1, next_ref[0][0:1, :], 0.0)
    u_prev = jnp.where(row == 0, prev_row, pltpu.roll(u, 1, 0))
    u_next = jnp.where(row == rows - 1, next_row, pltpu.roll(u, rows - 1, 0))
    cw = prm["conv"][...]
    rkv = u_prev * cw[0:1] + u * cw[1:2] + u_next * cw[2:3]
    r = rkv[:, :rw]
    k = rkv[:, rw:2 * rw]
    v = rkv[:, 2 * rw:]

    lo = lora_ref[0]
    wd = _bf(jnp.tanh(lo[:, :N_DIRS * DECAY_LORA]))
    ad = _bf(lo[:, N_DIRS * DECAY_LORA:N_DIRS * (DECAY_LORA + ICLR_LORA)])
    dsl = slice(d * rw, (d + 1) * rw)
    zw = prm["w0"][d:d + 1] + _dot(wd, prm["wup"][:, dsl])
    w_log = jnp.minimum(zw, 0.0) - jnp.log(1.0 + jnp.exp(-jnp.abs(zw))) - 0.5
    lw = -jnp.exp(w_log)
    a = _sigmoid(prm["a0"][d:d + 1] + _dot(ad, prm["aup"][:, dsl]))
    k_a = prm["k_a"][...]
    eseg = prm["eseg"][...]
    kk = k * prm["k_k"][...]
    ss = sum(_dot(p, eseg) for p in _split(kk * kk, 2))
    kk = kk / jnp.maximum(jnp.sqrt(ss), 1e-12)
    kd = k * (1.0 + (a - 1.0) * k_a)
    b = kk * a

    if bonus_ref is not None:
        osl = slice((1 - d) * rw, (2 - d) * rw)
        a_o = _sigmoid(prm["a0"][1 - d:2 - d] + _dot(ad, prm["aup"][:, osl]))
        kd_o = k * (1.0 + (a_o - 1.0) * k_a)
        rk = sum(_dot(p, eseg) for p in _split(r * prm["r_k"][...] * (kd + kd_o), 2))
        bonus_ref[0] = rk * v

    ti = lax.broadcasted_iota(jnp.int32, (rows, rows), 0)
    si = lax.broadcasted_iota(jnp.int32, (rows, rows), 1)
    earlier = (si >= ti) if reverse else (si <= ti)
    tri = _bf((earlier & ((ti // c) == (si // c))).astype(F32))
    incl = sum(_dot(tri, p) for p in _split(lw, 3))
    last = 0 if reverse else c - 1
    ends = [incl[j * c + last:j * c + last + 1] for j in range(rows // c)]
    total = jnp.concatenate([jnp.broadcast_to(e, (c, rw)) for e in ends], axis=0)
    p_end = jnp.concatenate([jnp.broadcast_to(jnp.exp(e), (c, rw)) for e in ends], axis=0)
    e_ninc = jnp.exp(-incl)
    e_tot = jnp.exp(total - incl)
    return dict(kkm=_pairs(kk * jnp.exp(incl - lw)), rp=_pairs(r * jnp.exp(incl)),
                km=_pairs(kd * e_ninc), bm=_pairs(b * e_ninc), khat=_pairs(kd * e_tot),
                bhat=_pairs(b * e_tot), v=_pairs(v), p_end=_pairs(p_end))


def _scan_kernel(rkv_f, prev_f, next_f, lora_f, rkv_b, prev_b, next_b, lora_b,
                 conv_ref, w0_ref, wup_ref, a0_ref, aup_ref, kk_ref, ka_ref, rk_ref, eseg_ref,
                 hf0_ref, hb0_ref, of_ref, ob_ref, bonus_ref, hf_ref, hb_ref, *, n_blocks):
    i = pl.program_id(1)
    c = CHUNK
    n = HEAD_DIM
    npr = N_PAIRS
    nsub = SUB_CHUNKS
    half = nsub * npr
    p2 = N_DIRS * half

    @pl.when(i == 0)
    def _():
        hf_ref[...] = hf0_ref[...]
        hb_ref[...] = hb0_ref[...]

    prm = dict(conv=conv_ref, w0=w0_ref, wup=wup_ref, a0=a0_ref, aup=aup_ref, k_k=kk_ref,
               k_a=ka_ref, r_k=rk_ref, eseg=eseg_ref)
    opf = _block_operands(i, n_blocks, False, 0, rkv_f, prev_f, next_f, lora_f, prm, bonus_ref)
    opb = _block_operands(n_blocks - 1 - i, n_blocks, True, 1, rkv_b, prev_b, next_b, lora_b, prm, None)
    op = {key: jnp.concatenate([opf[key], opb[key]], axis=0) for key in opf}

    shape = (p2, c, PAIR)
    rev = lax.broadcasted_iota(jnp.int32, shape, 0) >= half
    ti = lax.broadcasted_iota(jnp.int32, shape, 1)
    lane = lax.broadcasted_iota(jnp.int32, shape, 2)
    lane_hi = lane >= n
    si = jnp.where(lane_hi, lane - n, lane)
    age = jnp.where(rev, ti - si, si - ti)
    strict_m = age < 0
    incl_m = age <= 0
    eye = age == 0
    same_blk = {}
    kb = SUBLANES
    while kb <= c:
        same_blk[kb] = (ti // kb) == (si // kb)
        kb *= 2

    lane_hi_step = lax.broadcasted_iota(jnp.int32, (N_DIRS * npr, c, PAIR), 2) >= n

    def bd(y):
        hi = lane_hi if y.shape[0] == p2 else lane_hi_step
        return _bf(jnp.concatenate([jnp.where(hi, 0.0, y), jnp.where(hi, y, 0.0)], axis=1))

    def diag_blocks(full):
        return jnp.where(lane_hi, full[:, n:], full[:, :n])

    kkm = op["kkm"]
    rp = op["rp"]
    x2 = _bf(jnp.concatenate([kkm, rp], axis=1))
    ab2 = _bdot_nt(x2, bd(op["bm"]))
    ak2 = _bdot_nt(x2, bd(op["km"]))
    a_b = jnp.where(strict_m, ab2[:, :c], 0.0)
    a_rb = _bf(jnp.where(incl_m, ab2[:, c:], 0.0))
    a_k = _bf(jnp.where(strict_m, ak2[:, :c], 0.0))
    a_rk = _bf(jnp.where(incl_m, ak2[:, c:], 0.0))
    tinv = _bf(_tri_inverse(a_b, same_blk, eye.astype(F32), bd))
    v_bd = bd(op["v"])
    akv = _bdot(a_k, v_bd)
    wt = _bdot(tinv, bd(kkm))
    ut = _bdot(tinv, bd(akv))
    qt = _bf(rp - _bdot(a_rb, bd(wt)))
    ot = _bdot(a_rk, v_bd) - _bdot(a_rb, bd(ut))
    bhat_t = _bf(jnp.swapaxes(op["bhat"], 1, 2))
    khat_t = _bf(jnp.swapaxes(op["khat"], 1, 2))
    mc = _bf(diag_blocks(_bdot(bhat_t, _bf(wt))))
    g = diag_blocks(_bdot(khat_t, _bf(op["v"])) - _bdot(bhat_t, _bf(ut)))
    dec = jnp.where(eye, op["p_end"], 0.0)
    p_lo = jnp.sum(jnp.where(lane_hi, 0.0, dec), axis=2, keepdims=True)
    p_hi = jnp.sum(jnp.where(lane_hi, dec, 0.0), axis=2, keepdims=True)
    p_col = jnp.where(lane_hi, p_hi, p_lo)

    hh = jnp.concatenate([hf_ref[0], hb_ref[0]], axis=0)
    for step in range(nsub):
        kf = step
        kr = nsub - 1 - step

        def ent(x, kf=kf, kr=kr):
            return jnp.concatenate([x[kf * npr:(kf + 1) * npr],
                                    x[half + kr * npr:half + (kr + 1) * npr]], axis=0)

        h_bd = bd(hh)
        o = _bdot(ent(qt), h_bd) + ent(ot)
        hh = ent(p_col) * hh - _bdot(ent(mc), h_bd) + ent(g)
        for j in range(npr):
            ps = slice(j * PAIR, (j + 1) * PAIR)
            of_ref[0, kf * c:(kf + 1) * c, ps] = o[j]
            ob_ref[0, kr * c:(kr + 1) * c, ps] = o[npr + j]
    hf_ref[0] = hh[:npr]
    hb_ref[0] = hh[npr:]


def _scan(u_rkv, u_lora, hf0, hb0, conv, w0, wup_bd, a0, aup_bd, k_k, k_a, r_k, eseg):
    b, t, _ = u_rkv.shape
    blk = SUB_CHUNKS * CHUNK
    n = t // blk
    per8 = blk // SUBLANES
    last8 = t // SUBLANES - 1
    rw = RWKV_WIDTH

    def fwd(bi, i):
        return (bi, i, 0)

    def fwd_prev(bi, i):
        return (bi, jnp.maximum(i * per8 - 1, 0), 0)

    def fwd_next(bi, i):
        return (bi, jnp.minimum((i + 1) * per8, last8), 0)

    def bwd(bi, i):
        return (bi, n - 1 - i, 0)

    def bwd_prev(bi, i):
        return (bi, jnp.maximum((n - 1 - i) * per8 - 1, 0), 0)

    def bwd_next(bi, i):
        return (bi, jnp.minimum((n - i) * per8, last8), 0)

    def seq_specs(main, prev, nxt):
        return [pl.BlockSpec((1, blk, 3 * rw), main),
                pl.BlockSpec((1, SUBLANES, 3 * rw), prev),
                pl.BlockSpec((1, SUBLANES, 3 * rw), nxt),
                pl.BlockSpec((1, blk, LORA_W), main)]

    state_spec = pl.BlockSpec((1, N_PAIRS, HEAD_DIM, PAIR), lambda bi, i: (bi, 0, 0, 0))
    consts = (conv, w0, wup_bd, a0, aup_bd, k_k, k_a, r_k, eseg)
    seq_shape = jax.ShapeDtypeStruct((b, t, rw), F32)
    return pl.pallas_call(
        functools.partial(_scan_kernel, n_blocks=n),
        grid=(b, n),
        in_specs=(seq_specs(fwd, fwd_prev, fwd_next) + seq_specs(bwd, bwd_prev, bwd_next)
                  + [_const_spec(x.shape) for x in consts] + [state_spec, state_spec]),
        out_specs=[pl.BlockSpec((1, blk, rw), fwd), pl.BlockSpec((1, blk, rw), bwd),
                   pl.BlockSpec((1, blk, rw), fwd), state_spec, state_spec],
        out_shape=[seq_shape, seq_shape, seq_shape,
                   jax.ShapeDtypeStruct(hf0.shape, F32), jax.ShapeDtypeStruct(hb0.shape, F32)],
        compiler_params=_params(("arbitrary", "arbitrary")),
        name="rwkv7_scan",
    )(u_rkv, u_rkv, u_rkv, u_lora, u_rkv, u_rkv, u_rkv, u_lora, *consts, hf0, hb0)


def _dft_tables(n):
    ang = 2.0 * np.pi * (np.outer(np.arange(n), np.arange(n)) % n) / n
    return np.cos(ang), np.sin(ang)


def _table(x):
    return jnp.asarray(x, F32).astype(BF16)


def _four_ctx_kernel(f_ref, wc_ref, ct_ref, st_ref, o_ref):
    y = _bf(_dot(_bf(f_ref[0]), wc_ref[...]))
    w = FOURIER_WIDTH
    o_ref[0] = _dot(ct_ref[...], y[:, :w]) - _dot(st_ref[...], y[:, w:])


def _fourier_ctx(f):
    b, t, w = f.shape
    gd = FOURIER_GROUP_DIM
    cc, sc = _dft_tables(gd)
    wc = np.zeros((w, 2 * w), np.float32)
    for g in range(FOURIER_GROUPS):
        wc[g * gd:(g + 1) * gd, g * gd:(g + 1) * gd] = cc
        wc[g * gd:(g + 1) * gd, w + g * gd:w + (g + 1) * gd] = sc
    ct, st = _dft_tables(t)
    norm = 1.0 / np.sqrt(t * gd)
    consts = [_table(wc), _table(ct * norm), _table(st * norm)]
    return pl.pallas_call(
        _four_ctx_kernel,
        grid=(b,),
        in_specs=[pl.BlockSpec((1, t, w), lambda bi: (bi, 0, 0))] + [_const_spec(x.shape) for x in consts],
        out_specs=pl.BlockSpec((1, t, w), lambda bi: (bi, 0, 0)),
        out_shape=jax.ShapeDtypeStruct(f.shape, F32),
        compiler_params=_params(("arbitrary",)),
        name="fourier_ctx",
    )(f, *consts)


W_ROWS = 4


def _four_lat_kernel(f_ref, wc_ref, kcw_ref, ksw_ref, kcr_ref, ksr_ref, o_ref, z_ref):
    gd = FOURIER_GROUP_DIM
    rows = f_ref.shape[1]
    x = _bf(f_ref[0].reshape(rows * GRID_W, gd))
    y = _bf(_dot(x, wc_ref[...]))
    kcw = kcw_ref[...]
    ksw = ksw_ref[...]
    blk = W_ROWS * GRID_W
    for j in range(rows // W_ROWS):
        yy = y[j * blk:(j + 1) * blk]
        pc = _dot(kcw, yy)
        ps = _dot(ksw, yy)
        z_ref[j * W_ROWS:(j + 1) * W_ROWS, :, :gd] = (pc[:, :gd] - ps[:, gd:]).reshape(W_ROWS, GRID_W, gd)
        z_ref[j * W_ROWS:(j + 1) * W_ROWS, :, gd:] = (ps[:, :gd] + pc[:, gd:]).reshape(W_ROWS, GRID_W, gd)
    kcr = kcr_ref[...]
    ksr = ksr_ref[...]
    for j in range(GRID_W // SUBLANES):
        ws = slice(j * SUBLANES, (j + 1) * SUBLANES)
        zg = _bf(z_ref[:, ws, :].reshape(rows * SUBLANES, 2 * gd))
        re = _dot(kcr, zg)[:, :gd] - _dot(ksr, zg)[:, gd:]
        o_ref[0, :, ws, :] = re.reshape(rows, SUBLANES, gd)


def _fourier_lat(f):
    b, t, w = f.shape
    gd = FOURIER_GROUP_DIM
    rows = t // GRID_W
    cc, sc = _dft_tables(gd)
    cw, sw = _dft_tables(GRID_W)
    cr, sr = _dft_tables(rows)
    norm = 1.0 / np.sqrt(rows * GRID_W * gd)
    eye_w = np.eye(W_ROWS)
    eye_8 = np.eye(SUBLANES)
    consts = [_table(np.concatenate([cc, sc], axis=1)),
              _table(np.kron(eye_w, cw)), _table(np.kron(eye_w, sw)),
              _table(np.kron(cr, eye_8) * norm), _table(np.kron(sr, eye_8) * norm)]
    f4 = f.reshape(b, rows, GRID_W, w)
    blk = pl.BlockSpec((1, rows, GRID_W, gd), lambda bi, g: (bi, 0, 0, g))
    out = pl.pallas_call(
        _four_lat_kernel,
        grid=(b, FOURIER_GROUPS),
        in_specs=[blk] + [_const_spec(x.shape) for x in consts],
        out_specs=blk,
        out_shape=jax.ShapeDtypeStruct(f4.shape, F32),
        scratch_shapes=[pltpu.VMEM((rows, GRID_W, 2 * gd), F32)],
        compiler_params=_params(("arbitrary", "arbitrary")),
        name="fourier_lat",
    )(f4, *consts)
    return out.reshape(b, t, w)


def _mixout_kernel(x_ref, mod_ref, of_ref, ob_ref, bonus_ref, lora_ref, four_ref, gate_ref,
                   eseg_ref, lnw_ref, lnb_ref, gup_ref, gb_ref, wo_ref, wf_ref, wout_ref, o_ref):
    eseg = eseg_ref[...]
    inv_n = 1.0 / HEAD_DIM
    o = of_ref[0] + ob_ref[0]
    mu = sum(_dot(p, eseg) for p in _split(o, 2)) * inv_n
    dlt = o - mu
    var = sum(_dot(p, eseg) for p in _split(dlt * dlt, 2)) * inv_n
    on = dlt * lax.rsqrt(var + GN_EPS) * lnw_ref[...] + lnb_ref[...]
    gd = lora_ref[0][:, N_DIRS * (DECAY_LORA + ICLR_LORA):]
    g = _dot(_bf(_sigmoid(gd)), gup_ref[...])
    y_rwkv = _dot(_bf((on + bonus_ref[0]) * g), wo_ref[...])
    y_four = _dot(_bf(four_ref[0]), wf_ref[...])
    gp = gate_ref[0]
    gb = gb_ref[...]
    g0 = _sigmoid(gp[:, :D_MODEL] + gb[0:1])
    g1 = _sigmoid(gp[:, D_MODEL:] + gb[1:2])
    merged = g0 * y_rwkv + g1 * y_four
    y = _dot(_bf(merged), wout_ref[...])
    o_ref[0] = x_ref[0] + mod_ref[0][5:6] * y


def _mixout(x, mod, per_batch_mod, o_f, o_b, bonus, u_lora, four, gate_pre, consts, *, tm):
    b, t, d = x.shape
    mod_map = (lambda bi, i: (bi + 1, 0, 0)) if per_batch_mod else (lambda bi, i: (0, 0, 0))

    def seq(w):
        return pl.BlockSpec((1, tm, w), lambda bi, i: (bi, i, 0))

    return pl.pallas_call(
        _mixout_kernel,
        grid=(b, t // tm),
        in_specs=([seq(d), pl.BlockSpec((1, N_MOD, d), mod_map), seq(RWKV_WIDTH), seq(RWKV_WIDTH),
                   seq(RWKV_WIDTH), seq(LORA_W), seq(FOURIER_WIDTH), seq(N_BRANCHES * D_MODEL)]
                  + [_const_spec(c.shape) for c in consts]),
        out_specs=seq(d),
        out_shape=jax.ShapeDtypeStruct(x.shape, F32),
        compiler_params=_params(("arbitrary", "arbitrary")),
        name="mixer_out",
    )(x, mod, o_f, o_b, bonus, u_lora, four, gate_pre, *consts)


def _state_to_pairs(s):
    b = s.shape[0]
    s = s.reshape(b, N_PAIRS, 2, HEAD_DIM, HEAD_DIM)
    return jnp.transpose(s, (0, 1, 4, 2, 3)).reshape(b, N_PAIRS, HEAD_DIM, PAIR)


def _pairs_to_state(hp):
    b = hp.shape[0]
    hp = hp.reshape(b, N_PAIRS, HEAD_DIM, 2, HEAD_DIM)
    return jnp.transpose(hp, (0, 1, 3, 4, 2)).reshape(b, RWKV_HEADS, HEAD_DIM, HEAD_DIM)


def _block_diag_lora(w_up):
    dirs, rank, width = w_up.shape
    out = jnp.zeros((dirs * rank, dirs * width), F32)
    for d in range(dirs):
        out = out.at[d * rank:(d + 1) * rank, d * width:(d + 1) * width].set(w_up[d])
    return _bf(out)


def kernel(x_prompt, x_sample, state_rwkv_fwd, state_rwkv_bwd, c, c_ctx, ada_w, ada_b, norm_w,
           ffn1_w13, ffn1_w2, ffn2_w13, ffn2_w2, w_in, gate_b, conv_rkv, w0, w_up, a0, a_up, g_up,
           k_k, k_a, r_k, lnx_w, lnx_b, w_o_rwkv, w_fourier, w_out, final_norm):
    depth = ada_w.shape[0]
    b_p = x_prompt.shape[0]
    b_s = x_sample.shape[0]
    rw = RWKV_WIDTH
    seg = np.arange(rw) // HEAD_DIM
    eseg = jnp.asarray(seg[:, None] == seg[None, :], BF16)
    cc = jnp.zeros((2 * SUBLANES, D_MODEL), F32).at[0].set(c_ctx).at[1:1 + b_s].set(c)
    zero_state = jnp.zeros((b_p, N_PAIRS, HEAD_DIM, PAIR), F32)

    x_p, x_s = x_prompt, x_sample
    new_f, new_b = [], []
    for l in range(depth):
        mod = _modulation(cc, ada_w[l], ada_b[l])[:1 + b_s].reshape(1 + b_s, N_MOD, D_MODEL)
        w13_1, w2_1 = _bf(ffn1_w13[l]), _bf(ffn1_w2[l])
        w13_2, w2_2 = _bf(ffn2_w13[l]), _bf(ffn2_w2[l])
        w_in_b = _bf(w_in[l])
        scan_consts = (conv_rkv[l], w0[l], _block_diag_lora(w_up[l]), a0[l], _block_diag_lora(a_up[l]),
                       k_k[l].reshape(1, rw), k_a[l].reshape(1, rw), r_k[l].reshape(1, rw), eseg)
        mix_consts = (eseg, lnx_w[l].reshape(1, rw), lnx_b[l].reshape(1, rw), _bf(g_up[l]), gate_b[l],
                      _bf(w_o_rwkv[l]), _bf(w_fourier[l]), _bf(w_out[l]))
        last = l == depth - 1

        def layer(x, per_batch_mod, hf0, hb0, grid, tm):
            x = _ffn(x, mod, per_batch_mod, norm_w[l], w13_1, w2_1, final_norm,
                     mod_off=0, norm_row=0, final=False, tm=tm)
            u_rkv, u_lora, f, gate_pre = _inproj(x, mod, per_batch_mod, norm_w[l], w_in_b, tm=tm)
            o_f, o_b, bonus, hf, hb = _scan(u_rkv, u_lora, hf0, hb0, *scan_consts)
            four = _fourier_lat(f) if grid else _fourier_ctx(f)
            x = _mixout(x, mod, per_batch_mod, o_f, o_b, bonus, u_lora, four, gate_pre, mix_consts, tm=tm)
            x = _ffn(x, mod, per_batch_mod, norm_w[l], w13_2, w2_2, final_norm,
                     mod_off=6, norm_row=2, final=last, tm=tm)
            return x, hf, hb

        x_p, hf, hb = layer(x_p, False, zero_state, zero_state, False, 256)
        new_f.append(_pairs_to_state(hf))
        new_b.append(_pairs_to_state(hb))
        hf0 = _state_to_pairs(state_rwkv_fwd[:, l])
        hb0 = _state_to_pairs(state_rwkv_bwd[:, l])
        x_s, _, _ = layer(x_s, True, hf0, hb0, True, 512)
    if depth == 0:
        raise ValueError("depth must be positive")
    return (x_p, x_s, jnp.stack(new_f, axis=1), jnp.stack(new_b, axis=1))
```

```python
import functools

import numpy as np
import jax
import jax.numpy as jnp
from jax import lax
from jax.experimental import pallas as pl
from jax.experimental.pallas import tpu as pltpu

D_MODEL = 1024
RWKV_HEADS = 8
HEAD_DIM = 64
RWKV_WIDTH = RWKV_HEADS * HEAD_DIM
DECAY_LORA = 64
ICLR_LORA = 64
GATE_LORA = 128
N_DIRS = 2
FOURIER_GROUPS = 4
FOURIER_GROUP_DIM = 128
FOURIER_WIDTH = FOURIER_GROUPS * FOURIER_GROUP_DIM
GRID_W = 64
N_BRANCHES = 2
D_FF = 2816
N_MOD = 9
NORM_EPS = 1e-6
GN_EPS = 64e-5

OFF_WD = 3 * RWKV_WIDTH
LORA_W = N_DIRS * DECAY_LORA + N_DIRS * ICLR_LORA + GATE_LORA
OFF_F = OFF_WD + LORA_W
OFF_GATE = OFF_F + FOURIER_WIDTH
PROJ_W = OFF_GATE + N_BRANCHES * D_MODEL

CHUNK = 64
PAIR = 2 * HEAD_DIM
N_PAIRS = RWKV_HEADS // 2
SUB_CHUNKS = 2
SUBLANES = 8
FF_CHUNK = 1408
VMEM_LIMIT = 56 * 1024 * 1024

F32 = jnp.float32
BF16 = jnp.bfloat16


def _bf(x):
    return x.astype(BF16)


def _dot(a, b):
    return jnp.dot(a, b, preferred_element_type=F32)


def _split(x, parts):
    out = []
    rem = x
    for _ in range(parts):
        p = rem.astype(BF16)
        out.append(p)
        rem = rem - p.astype(F32)
    return out


def _head_sums(x, e_pair, parts=2):
    cols = []
    for j in range(x.shape[1] // PAIR):
        xs = x[:, j * PAIR:(j + 1) * PAIR]
        cols.append(sum(_dot(p, e_pair) for p in _split(xs, parts)))
    return jnp.concatenate(cols, axis=1)


def _sigmoid(x):
    return 1.0 / (1.0 + jnp.exp(-x))


def _rms(x, w):
    ms = jnp.mean(x * x, axis=-1, keepdims=True)
    return x * lax.rsqrt(ms + NORM_EPS) * w


def _const_spec(shape):
    nd = len(shape)
    return pl.BlockSpec(shape, lambda *_: (0,) * nd, pipeline_mode=pl.Buffered(1))


def _params(sem):
    return pltpu.CompilerParams(dimension_semantics=sem, vmem_limit_bytes=VMEM_LIMIT)


def _mod_kernel(c_ref, w_ref, b_ref, o_ref):
    c = c_ref[...]
    s = c * _sigmoid(c)
    o_ref[...] = _dot(_bf(s), _bf(w_ref[...])) + b_ref[...]


def _modulation(cc, ada_w, ada_b):
    rows, d = cc.shape
    n = ada_w.shape[1]
    tn = n // 8
    return pl.pallas_call(
        _mod_kernel,
        grid=(n // tn,),
        in_specs=[pl.BlockSpec((rows, d), lambda j: (0, 0)),
                  pl.BlockSpec((d, tn), lambda j: (0, j)),
                  pl.BlockSpec((1, tn), lambda j: (0, j))],
        out_specs=pl.BlockSpec((rows, tn), lambda j: (0, j)),
        out_shape=jax.ShapeDtypeStruct((rows, n), F32),
        compiler_params=_params(("arbitrary",)),
        name="adaln_mod",
    )(cc, ada_w, ada_b.reshape(1, n))


def _ffn_kernel(x_ref, mod_ref, nw_ref, w13_ref, w2_ref, fn_ref, o_ref, *, mod_off, norm_row, final):
    x = x_ref[0]
    m = mod_ref[0]
    shift = m[mod_off:mod_off + 1]
    scale = m[mod_off + 1:mod_off + 2]
    gate = m[mod_off + 2:mod_off + 3]
    h = _rms(x, nw_ref[norm_row:norm_row + 1]) * (1.0 + scale) + shift
    hb = _bf(h)
    acc = jnp.zeros(x.shape, F32)
    for j in range(D_FF // FF_CHUNK):
        a = _dot(hb, w13_ref[:, j * FF_CHUNK:(j + 1) * FF_CHUNK])
        b = _dot(hb, w13_ref[:, D_FF + j * FF_CHUNK:D_FF + (j + 1) * FF_CHUNK])
        g = (a * _sigmoid(a)) * b
        acc = acc + _dot(_bf(g), w2_ref[j * FF_CHUNK:(j + 1) * FF_CHUNK, :])
    y = x + 0.5 * gate * acc
    if final:
        y = _rms(y, fn_ref[...])
    o_ref[0] = y


def _ffn(x, mod, per_batch_mod, norm_w, w13, w2, final_norm, *, mod_off, norm_row, final, tm):
    b, t, d = x.shape
    mod_map = (lambda bi, i: (bi + 1, 0, 0)) if per_batch_mod else (lambda bi, i: (0, 0, 0))
    kern = functools.partial(_ffn_kernel, mod_off=mod_off, norm_row=norm_row, final=final)
    return pl.pallas_call(
        kern,
        grid=(b, t // tm),
        in_specs=[pl.BlockSpec((1, tm, d), lambda bi, i: (bi, i, 0)),
                  pl.BlockSpec((1, N_MOD, d), mod_map),
                  _const_spec(norm_w.shape),
                  _const_spec(w13.shape),
                  _const_spec(w2.shape),
                  _const_spec((1, d))],
        out_specs=pl.BlockSpec((1, tm, d), lambda bi, i: (bi, i, 0)),
        out_shape=jax.ShapeDtypeStruct(x.shape, F32),
        compiler_params=_params(("arbitrary", "arbitrary")),
        name="ffn_final" if final else "ffn",
    )(x, mod, norm_w, w13, w2, final_norm.reshape(1, d))


def _inproj_kernel(x_ref, mod_ref, nw_ref, w_ref, rkv_ref, lora_ref, f_ref, gate_ref):
    x = x_ref[0]
    m = mod_ref[0]
    h = _rms(x, nw_ref[1:2]) * (1.0 + m[4:5]) + m[3:4]
    hb = _bf(h)
    rkv_ref[0] = _dot(hb, w_ref[:, :OFF_WD])
    lora_ref[0] = _dot(hb, w_ref[:, OFF_WD:OFF_F])
    f_ref[0] = _dot(hb, w_ref[:, OFF_F:OFF_GATE])
    gate_ref[0] = _dot(hb, w_ref[:, OFF_GATE:])


def _inproj(x, mod, per_batch_mod, norm_w, w_in, *, tm):
    b, t, d = x.shape
    mod_map = (lambda bi, i: (bi + 1, 0, 0)) if per_batch_mod else (lambda bi, i: (0, 0, 0))
    widths = (OFF_WD, LORA_W, FOURIER_WIDTH, N_BRANCHES * D_MODEL)
    return pl.pallas_call(
        _inproj_kernel,
        grid=(b, t // tm),
        in_specs=[pl.BlockSpec((1, tm, d), lambda bi, i: (bi, i, 0)),
                  pl.BlockSpec((1, N_MOD, d), mod_map),
                  _const_spec(norm_w.shape),
                  _const_spec(w_in.shape)],
        out_specs=[pl.BlockSpec((1, tm, w), lambda bi, i: (bi, i, 0)) for w in widths],
        out_shape=[jax.ShapeDtypeStruct((b, t, w), F32) for w in widths],
        compiler_params=_params(("arbitrary", "arbitrary")),
        name="mixer_inproj",
    )(x, mod, norm_w, w_in)


def _bdot(a, b):
    return lax.dot_general(a, b, (((2,), (1,)), ((0,), (0,))), preferred_element_type=F32)


def _bdot_nt(a, b):
    return lax.dot_general(a, b, (((2,), (2,)), ((0,), (0,))), preferred_element_type=F32)


def _tri_inverse(ab, same_blk, eye_f, bd):
    c = ab.shape[1]
    n1 = jnp.where(same_blk[SUBLANES], ab, 0.0)
    n2 = _bdot(_bf(n1), bd(n1))
    n4 = _bdot(_bf(n2), bd(n2))
    t = _bdot(_bf(eye_f - n1), bd(eye_f + n2))
    t = _bdot(_bf(t), bd(eye_f + n4))
    k = SUBLANES
    while k < c:
        off = jnp.where(same_blk[2 * k] & jnp.logical_not(same_blk[k]), ab, 0.0)
        t = t - _bdot(_bf(t), bd(_bdot(_bf(off), bd(t))))
        k *= 2
    return t


def _pairs(x):
    return jnp.stack([x[k * CHUNK:(k + 1) * CHUNK, j * PAIR:(j + 1) * PAIR]
                      for k in range(x.shape[0] // CHUNK) for j in range(N_PAIRS)], axis=0)


def _block_operands(bi, n_blocks, reverse, d, rkv_ref, prev_ref, next_ref, lora_ref, prm, bonus_ref):
    c = CHUNK
    rw = RWKV_WIDTH
    u = rkv_ref[0]
    rows = u.shape[0]
    row = lax.broadcasted_iota(jnp.int32, (rows, 1), 0)
    prev_row = jnp.where(bi > 0, prev_ref[0][SUBLANES - 1:SUBLANES, :], 0.0)
    next_row = jnp.where(bi < n_blocks - 1, next_ref[0][0:1, :], 0.0)
    u_prev = jnp.where(row == 0, prev_row, pltpu.roll(u, 1, 0))
    u_next = jnp.where(row == rows - 1, next_row, pltpu.roll(u, rows - 1, 0))
    cw = prm["conv"][...]
    rkv = u_prev * cw[0:1] + u * cw[1:2] + u_next * cw[2:3]
    r = rkv[:, :rw]
    k = rkv[:, rw:2 * rw]
    v = rkv[:, 2 * rw:]

    lo = lora_ref[0]
    wd = _bf(jnp.tanh(lo[:, :N_DIRS * DECAY_LORA]))
    ad = _bf(lo[:, N_DIRS * DECAY_LORA:N_DIRS * (DECAY_LORA + ICLR_LORA)])
    dsl = slice(d * rw, (d + 1) * rw)
    zw = prm["w0"][d:d + 1] + _dot(wd, prm["wup"][:, dsl])
    w_log = jnp.minimum(zw, 0.0) - jnp.log(1.0 + jnp.exp(-jnp.abs(zw))) - 0.5
    lw = -jnp.exp(w_log)
    a = _sigmoid(prm["a0"][d:d + 1] + _dot(ad, prm["aup"][:, dsl]))
    k_a = prm["k_a"][...]
    eseg = prm["eseg"][...]
    kk = k * prm["k_k"][...]
    ss = _head_sums(kk * kk, eseg)
    kk = kk / jnp.maximum(jnp.sqrt(ss), 1e-12)
    kd = k * (1.0 + (a - 1.0) * k_a)
    b = kk * a

    if bonus_ref is not None:
        osl = slice((1 - d) * rw, (2 - d) * rw)
        a_o = _sigmoid(prm["a0"][1 - d:2 - d] + _dot(ad, prm["aup"][:, osl]))
        kd_o = k * (1.0 + (a_o - 1.0) * k_a)
        rk = _head_sums(r * prm["r_k"][...] * (kd + kd_o), eseg)
        bonus_ref[0] = rk * v

    ti = lax.broadcasted_iota(jnp.int32, (rows, rows), 0)
    si = lax.broadcasted_iota(jnp.int32, (rows, rows), 1)
    earlier = (si >= ti) if reverse else (si <= ti)
    tri = _bf((earlier & ((ti // c) == (si // c))).astype(F32))
    incl = sum(_dot(tri, p) for p in _split(lw, 3))
    last = 0 if reverse else c - 1
    ends = [incl[j * c + last:j * c + last + 1] for j in range(rows // c)]
    total = jnp.concatenate([jnp.broadcast_to(e, (c, rw)) for e in ends], axis=0)
    p_end = jnp.concatenate([jnp.broadcast_to(jnp.exp(e), (c, rw)) for e in ends], axis=0)
    e_ninc = jnp.exp(-incl)
    e_tot = jnp.exp(total - incl)
    return dict(kkm=_pairs(kk * jnp.exp(incl - lw)), rp=_pairs(r * jnp.exp(incl)),
                km=_pairs(kd * e_ninc), bm=_pairs(b * e_ninc), khat=_pairs(kd * e_tot),
                bhat=_pairs(b * e_tot), v=_pairs(v), p_end=_pairs(p_end))


def _scan_kernel(rkv_f, prev_f, next_f, lora_f, rkv_b, prev_b, next_b, lora_b,
                 conv_ref, w0_ref, wup_ref, a0_ref, aup_ref, kk_ref, ka_ref, rk_ref, eseg_ref,
                 hf0_ref, hb0_ref, of_ref, ob_ref, bonus_ref, hf_ref, hb_ref, *, n_blocks):
    i = pl.program_id(1)
    c = CHUNK
    n = HEAD_DIM
    npr = N_PAIRS
    nsub = SUB_CHUNKS
    half = nsub * npr
    p2 = N_DIRS * half

    @pl.when(i == 0)
    def _():
        hf_ref[...] = hf0_ref[...]
        hb_ref[...] = hb0_ref[...]

    prm = dict(conv=conv_ref, w0=w0_ref, wup=wup_ref, a0=a0_ref, aup=aup_ref, k_k=kk_ref,
               k_a=ka_ref, r_k=rk_ref, eseg=eseg_ref)
    opf = _block_operands(i, n_blocks, False, 0, rkv_f, prev_f, next_f, lora_f, prm, bonus_ref)
    opb = _block_operands(n_blocks - 1 - i, n_blocks, True, 1, rkv_b, prev_b, next_b, lora_b, prm, None)
    op = {key: jnp.concatenate([opf[key], opb[key]], axis=0) for key in opf}

    shape = (p2, c, PAIR)
    rev = lax.broadcasted_iota(jnp.int32, shape, 0) >= half
    ti = lax.broadcasted_iota(jnp.int32, shape, 1)
    lane = lax.broadcasted_iota(jnp.int32, shape, 2)
    lane_hi = lane >= n
    si = jnp.where(lane_hi, lane - n, lane)
    age = jnp.where(rev, ti - si, si - ti)
    strict_m = age < 0
    incl_m = age <= 0
    eye = age == 0
    same_blk = {}
    kb = SUBLANES
    while kb <= c:
        same_blk[kb] = (ti // kb) == (si // kb)
        kb *= 2

    lane_hi_step = lax.broadcasted_iota(jnp.int32, (N_DIRS * npr, c, PAIR), 2) >= n

    def bd(y):
        hi = lane_hi if y.shape[0] == p2 else lane_hi_step
        return _bf(jnp.concatenate([jnp.where(hi, 0.0, y), jnp.where(hi, y, 0.0)], axis=1))

    def diag_blocks(full):
        return jnp.where(lane_hi, full[:, n:], full[:, :n])

    kkm = op["kkm"]
    rp = op["rp"]
    x2 = _bf(jnp.concatenate([kkm, rp], axis=1))
    ab2 = _bdot_nt(x2, bd(op["bm"]))
    ak2 = _bdot_nt(x2, bd(op["km"]))
    a_b = jnp.where(strict_m, ab2[:, :c], 0.0)
    a_rb = _bf(jnp.where(incl_m, ab2[:, c:], 0.0))
    a_k = _bf(jnp.where(strict_m, ak2[:, :c], 0.0))
    a_rk = _bf(jnp.where(incl_m, ak2[:, c:], 0.0))
    tinv = _bf(_tri_inverse(a_b, same_blk, eye.astype(F32), bd))
    v_bd = bd(op["v"])
    av = _bdot(jnp.concatenate([a_k, a_rk], axis=1), v_bd)
    akv = av[:, :c]
    wt = _bdot(tinv, bd(kkm))
    ut = _bdot(tinv, bd(akv))
    qt = _bf(rp - _bdot(a_rb, bd(wt)))
    ot = av[:, c:] - _bdot(a_rb, bd(ut))
    bhat_t = _bf(jnp.swapaxes(op["bhat"], 1, 2))
    khat_t = _bf(jnp.swapaxes(op["khat"], 1, 2))
    mc = _bf(diag_blocks(_bdot(bhat_t, _bf(wt))))
    g = diag_blocks(_bdot(khat_t, _bf(op["v"])) - _bdot(bhat_t, _bf(ut)))
    dec = jnp.where(eye, op["p_end"], 0.0)
    p_lo = jnp.sum(jnp.where(lane_hi, 0.0, dec), axis=2, keepdims=True)
    p_hi = jnp.sum(jnp.where(lane_hi, dec, 0.0), axis=2, keepdims=True)
    p_col = jnp.where(lane_hi, p_hi, p_lo)
    qm = jnp.concatenate([qt, mc], axis=1)

    hh = jnp.concatenate([hf_ref[0], hb_ref[0]], axis=0)
    for step in range(nsub):
        kf = step
        kr = nsub - 1 - step

        def ent(x, kf=kf, kr=kr):
            return jnp.concatenate([x[kf * npr:(kf + 1) * npr],
                                    x[half + kr * npr:half + (kr + 1) * npr]], axis=0)

        qmh = _bdot(ent(qm), bd(hh))
        o = qmh[:, :c] + ent(ot)
        hh = ent(p_col) * hh - qmh[:, c:] + ent(g)
        for j in range(npr):
            ps = slice(j * PAIR, (j + 1) * PAIR)
            of_ref[0, kf * c:(kf + 1) * c, ps] = o[j]
            ob_ref[0, kr * c:(kr + 1) * c, ps] = o[npr + j]
    hf_ref[0] = hh[:npr]
    hb_ref[0] = hh[npr:]


def _scan(u_rkv, u_lora, hf0, hb0, conv, w0, wup_bd, a0, aup_bd, k_k, k_a, r_k, eseg):
    b, t, _ = u_rkv.shape
    blk = SUB_CHUNKS * CHUNK
    n = t // blk
    per8 = blk // SUBLANES
    last8 = t // SUBLANES - 1
    rw = RWKV_WIDTH

    def fwd(bi, i):
        return (bi, i, 0)

    def fwd_prev(bi, i):
        return (bi, jnp.maximum(i * per8 - 1, 0), 0)

    def fwd_next(bi, i):
        return (bi, jnp.minimum((i + 1) * per8, last8), 0)

    def bwd(bi, i):
        return (bi, n - 1 - i, 0)

    def bwd_prev(bi, i):
        return (bi, jnp.maximum((n - 1 - i) * per8 - 1, 0), 0)

    def bwd_next(bi, i):
        return (bi, jnp.minimum((n - i) * per8, last8), 0)

    def seq_specs(main, prev, nxt):
        return [pl.BlockSpec((1, blk, 3 * rw), main),
                pl.BlockSpec((1, SUBLANES, 3 * rw), prev),
                pl.BlockSpec((1, SUBLANES, 3 * rw), nxt),
                pl.BlockSpec((1, blk, LORA_W), main)]

    state_spec = pl.BlockSpec((1, N_PAIRS, HEAD_DIM, PAIR), lambda bi, i: (bi, 0, 0, 0))
    consts = (conv, w0, wup_bd, a0, aup_bd, k_k, k_a, r_k, eseg)
    seq_shape = jax.ShapeDtypeStruct((b, t, rw), F32)
    return pl.pallas_call(
        functools.partial(_scan_kernel, n_blocks=n),
        grid=(b, n),
        in_specs=(seq_specs(fwd, fwd_prev, fwd_next) + seq_specs(bwd, bwd_prev, bwd_next)
                  + [_const_spec(x.shape) for x in consts] + [state_spec, state_spec]),
        out_specs=[pl.BlockSpec((1, blk, rw), fwd), pl.BlockSpec((1, blk, rw), bwd),
                   pl.BlockSpec((1, blk, rw), fwd), state_spec, state_spec],
        out_shape=[seq_shape, seq_shape, seq_shape,
                   jax.ShapeDtypeStruct(hf0.shape, F32), jax.ShapeDtypeStruct(hb0.shape, F32)],
        compiler_params=_params(("arbitrary", "arbitrary")),
        name="rwkv7_scan",
    )(u_rkv, u_rkv, u_rkv, u_lora, u_rkv, u_rkv, u_rkv, u_lora, *consts, hf0, hb0)


def _dft_tables(n):
    ang = 2.0 * np.pi * (np.outer(np.arange(n), np.arange(n)) % n) / n
    return np.cos(ang), np.sin(ang)


def _table(x):
    return jnp.asarray(x, F32).astype(BF16)


def _four_ctx_kernel(f_ref, wc_ref, ct_ref, st_ref, o_ref):
    y = _bf(_dot(_bf(f_ref[0]), wc_ref[...]))
    w = FOURIER_WIDTH
    o_ref[0] = _dot(ct_ref[...], y[:, :w]) - _dot(st_ref[...], y[:, w:])


def _fourier_ctx(f):
    b, t, w = f.shape
    gd = FOURIER_GROUP_DIM
    cc, sc = _dft_tables(gd)
    wc = np.zeros((w, 2 * w), np.float32)
    for g in range(FOURIER_GROUPS):
        wc[g * gd:(g + 1) * gd, g * gd:(g + 1) * gd] = cc
        wc[g * gd:(g + 1) * gd, w + g * gd:w + (g + 1) * gd] = sc
    ct, st = _dft_tables(t)
    norm = 1.0 / np.sqrt(t * gd)
    consts = [_table(wc), _table(ct * norm), _table(st * norm)]
    return pl.pallas_call(
        _four_ctx_kernel,
        grid=(b,),
        in_specs=[pl.BlockSpec((1, t, w), lambda bi: (bi, 0, 0))] + [_const_spec(x.shape) for x in consts],
        out_specs=pl.BlockSpec((1, t, w), lambda bi: (bi, 0, 0)),
        out_shape=jax.ShapeDtypeStruct(f.shape, F32),
        compiler_params=_params(("arbitrary",)),
        name="fourier_ctx",
    )(f, *consts)


W_ROWS = 4


def _four_lat_kernel(f_ref, wc_ref, kcw_ref, ksw_ref, kcr_ref, ksr_ref, o_ref, z_ref):
    gd = FOURIER_GROUP_DIM
    rows = f_ref.shape[1]
    x = _bf(f_ref[0].reshape(rows * GRID_W, gd))
    y = _bf(_dot(x, wc_ref[...]))
    kcw = kcw_ref[...]
    ksw = ksw_ref[...]
    blk = W_ROWS * GRID_W
    for j in range(rows // W_ROWS):
        yy = y[j * blk:(j + 1) * blk]
        pc = _dot(kcw, yy)
        ps = _dot(ksw, yy)
        z_ref[j * W_ROWS:(j + 1) * W_ROWS, :, :gd] = (pc[:, :gd] - ps[:, gd:]).reshape(W_ROWS, GRID_W, gd)
        z_ref[j * W_ROWS:(j + 1) * W_ROWS, :, gd:] = (ps[:, :gd] + pc[:, gd:]).reshape(W_ROWS, GRID_W, gd)
    kcr = kcr_ref[...]
    ksr = ksr_ref[...]
    for j in range(GRID_W // SUBLANES):
        ws = slice(j * SUBLANES, (j + 1) * SUBLANES)
        zg = _bf(z_ref[:, ws, :].reshape(rows * SUBLANES, 2 * gd))
        re = _dot(kcr, zg)[:, :gd] - _dot(ksr, zg)[:, gd:]
        o_ref[0, :, ws, :] = re.reshape(rows, SUBLANES, gd)


def _fourier_lat(f):
    b, t, w = f.shape
    gd = FOURIER_GROUP_DIM
    rows = t // GRID_W
    cc, sc = _dft_tables(gd)
    cw, sw = _dft_tables(GRID_W)
    cr, sr = _dft_tables(rows)
    norm = 1.0 / np.sqrt(rows * GRID_W * gd)
    eye_w = np.eye(W_ROWS)
    eye_8 = np.eye(SUBLANES)
    consts = [_table(np.concatenate([cc, sc], axis=1)),
              _table(np.kron(eye_w, cw)), _table(np.kron(eye_w, sw)),
              _table(np.kron(cr, eye_8) * norm), _table(np.kron(sr, eye_8) * norm)]
    f4 = f.reshape(b, rows, GRID_W, w)
    blk = pl.BlockSpec((1, rows, GRID_W, gd), lambda bi, g: (bi, 0, 0, g))
    out = pl.pallas_call(
        _four_lat_kernel,
        grid=(b, FOURIER_GROUPS),
        in_specs=[blk] + [_const_spec(x.shape) for x in consts],
        out_specs=blk,
        out_shape=jax.ShapeDtypeStruct(f4.shape, F32),
        scratch_shapes=[pltpu.VMEM((rows, GRID_W, 2 * gd), F32)],
        compiler_params=_params(("arbitrary", "arbitrary")),
        name="fourier_lat",
    )(f4, *consts)
    return out.reshape(b, t, w)


def _mixout_kernel(x_ref, mod_ref, of_ref, ob_ref, bonus_ref, lora_ref, four_ref, gate_ref,
                   eseg_ref, lnw_ref, lnb_ref, gup_ref, gb_ref, wo_ref, wf_ref, wout_ref, o_ref):
    eseg = eseg_ref[...]
    inv_n = 1.0 / HEAD_DIM
    o = of_ref[0] + ob_ref[0]
    mu = _head_sums(o, eseg) * inv_n
    dlt = o - mu
    var = _head_sums(dlt * dlt, eseg) * inv_n
    on = dlt * lax.rsqrt(var + GN_EPS) * lnw_ref[...] + lnb_ref[...]
    gd = lora_ref[0][:, N_DIRS * (DECAY_LORA + ICLR_LORA):]
    g = _dot(_bf(_sigmoid(gd)), gup_ref[...])
    y_rwkv = _dot(_bf((on + bonus_ref[0]) * g), wo_ref[...])
    y_four = _dot(_bf(four_ref[0]), wf_ref[...])
    gp = gate_ref[0]
    gb = gb_ref[...]
    g0 = _sigmoid(gp[:, :D_MODEL] + gb[0:1])
    g1 = _sigmoid(gp[:, D_MODEL:] + gb[1:2])
    merged = g0 * y_rwkv + g1 * y_four
    y = _dot(_bf(merged), wout_ref[...])
    o_ref[0] = x_ref[0] + mod_ref[0][5:6] * y


def _mixout(x, mod, per_batch_mod, o_f, o_b, bonus, u_lora, four, gate_pre, consts, *, tm):
    b, t, d = x.shape
    mod_map = (lambda bi, i: (bi + 1, 0, 0)) if per_batch_mod else (lambda bi, i: (0, 0, 0))

    def seq(w):
        return pl.BlockSpec((1, tm, w), lambda bi, i: (bi, i, 0))

    return pl.pallas_call(
        _mixout_kernel,
        grid=(b, t // tm),
        in_specs=([seq(d), pl.BlockSpec((1, N_MOD, d), mod_map), seq(RWKV_WIDTH), seq(RWKV_WIDTH),
                   seq(RWKV_WIDTH), seq(LORA_W), seq(FOURIER_WIDTH), seq(N_BRANCHES * D_MODEL)]
                  + [_const_spec(c.shape) for c in consts]),
        out_specs=seq(d),
        out_shape=jax.ShapeDtypeStruct(x.shape, F32),
        compiler_params=_params(("arbitrary", "arbitrary")),
        name="mixer_out",
    )(x, mod, o_f, o_b, bonus, u_lora, four, gate_pre, *consts)


def _state_to_pairs(s):
    b = s.shape[0]
    s = s.reshape(b, N_PAIRS, 2, HEAD_DIM, HEAD_DIM)
    return jnp.transpose(s, (0, 1, 4, 2, 3)).reshape(b, N_PAIRS, HEAD_DIM, PAIR)


def _pairs_to_state(hp):
    b = hp.shape[0]
    hp = hp.reshape(b, N_PAIRS, HEAD_DIM, 2, HEAD_DIM)
    return jnp.transpose(hp, (0, 1, 3, 4, 2)).reshape(b, RWKV_HEADS, HEAD_DIM, HEAD_DIM)


def _block_diag_lora(w_up):
    dirs, rank, width = w_up.shape
    out = jnp.zeros((dirs * rank, dirs * width), F32)
    for d in range(dirs):
        out = out.at[d * rank:(d + 1) * rank, d * width:(d + 1) * width].set(w_up[d])
    return _bf(out)


def kernel(x_prompt, x_sample, state_rwkv_fwd, state_rwkv_bwd, c, c_ctx, ada_w, ada_b, norm_w,
           ffn1_w13, ffn1_w2, ffn2_w13, ffn2_w2, w_in, gate_b, conv_rkv, w0, w_up, a0, a_up, g_up,
           k_k, k_a, r_k, lnx_w, lnx_b, w_o_rwkv, w_fourier, w_out, final_norm):
    depth = ada_w.shape[0]
    b_p = x_prompt.shape[0]
    b_s = x_sample.shape[0]
    rw = RWKV_WIDTH
    seg = np.arange(PAIR) // HEAD_DIM
    eseg = jnp.asarray(seg[:, None] == seg[None, :], BF16)
    cc = jnp.zeros((2 * SUBLANES, D_MODEL), F32).at[0].set(c_ctx).at[1:1 + b_s].set(c)
    zero_state = jnp.zeros((b_p, N_PAIRS, HEAD_DIM, PAIR), F32)

    x_p, x_s = x_prompt, x_sample
    new_f, new_b = [], []
    for l in range(depth):
        mod = _modulation(cc, ada_w[l], ada_b[l])[:1 + b_s].reshape(1 + b_s, N_MOD, D_MODEL)
        w13_1, w2_1 = _bf(ffn1_w13[l]), _bf(ffn1_w2[l])
        w13_2, w2_2 = _bf(ffn2_w13[l]), _bf(ffn2_w2[l])
        w_in_b = _bf(w_in[l])
        scan_consts = (conv_rkv[l], w0[l], _block_diag_lora(w_up[l]), a0[l], _block_diag_lora(a_up[l]),
                       k_k[l].reshape(1, rw), k_a[l].reshape(1, rw), r_k[l].reshape(1, rw), eseg)
        mix_consts = (eseg, lnx_w[l].reshape(1, rw), lnx_b[l].reshape(1, rw), _bf(g_up[l]), gate_b[l],
                      _bf(w_o_rwkv[l]), _bf(w_fourier[l]), _bf(w_out[l]))
        last = l == depth - 1

        def layer(x, per_batch_mod, hf0, hb0, grid, tm):
            x = _ffn(x, mod, per_batch_mod, norm_w[l], w13_1, w2_1, final_norm,
                     mod_off=0, norm_row=0, final=False, tm=tm)
            u_rkv, u_lora, f, gate_pre = _inproj(x, mod, per_batch_mod, norm_w[l], w_in_b, tm=tm)
            o_f, o_b, bonus, hf, hb = _scan(u_rkv, u_lora, hf0, hb0, *scan_consts)
            four = _fourier_lat(f) if grid else _fourier_ctx(f)
            x = _mixout(x, mod, per_batch_mod, o_f, o_b, bonus, u_lora, four, gate_pre, mix_consts, tm=tm)
            x = _ffn(x, mod, per_batch_mod, norm_w[l], w13_2, w2_2, final_norm,
                     mod_off=6, norm_row=2, final=last, tm=tm)
            return x, hf, hb

        x_p, hf, hb = layer(x_p, False, zero_state, zero_state, False, 256)
        new_f.append(_pairs_to_state(hf))
        new_b.append(_pairs_to_state(hb))
        hf0 = _state_to_pairs(state_rwkv_fwd[:, l])
        hb0 = _state_to_pairs(state_rwkv_bwd[:, l])
        x_s, _, _ = layer(x_s, True, hf0, hb0, True, 512)
    if depth == 0:
        raise ValueError("depth must be positive")
    return (x_p, x_s, jnp.stack(new_f, axis=1), jnp.stack(new_b, axis=1))
```

```python
import functools

import numpy as np
import jax
import jax.numpy as jnp
from jax import lax
from jax.experimental import pallas as pl
from jax.experimental.pallas import tpu as pltpu

D_MODEL = 1024
RWKV_HEADS = 8
HEAD_DIM = 64
RWKV_WIDTH = RWKV_HEADS * HEAD_DIM
DECAY_LORA = 64
ICLR_LORA = 64
GATE_LORA = 128
N_DIRS = 2
FOURIER_GROUPS = 4
FOURIER_GROUP_DIM = 128
FOURIER_WIDTH = FOURIER_GROUPS * FOURIER_GROUP_DIM
GRID_W = 64
N_BRANCHES = 2
D_FF = 2816
N_MOD = 9
NORM_EPS = 1e-6
GN_EPS = 64e-5

OFF_WD = 3 * RWKV_WIDTH
LORA_W = N_DIRS * DECAY_LORA + N_DIRS * ICLR_LORA + GATE_LORA
OFF_F = OFF_WD + LORA_W
OFF_GATE = OFF_F + FOURIER_WIDTH
PROJ_W = OFF_GATE + N_BRANCHES * D_MODEL

CHUNK = 64
PAIR = 2 * HEAD_DIM
N_PAIRS = RWKV_HEADS // 2
SUB_CHUNKS = 2
SUBLANES = 8
FF_CHUNK = 256
VMEM_LIMIT = 56 * 1024 * 1024

F32 = jnp.float32
BF16 = jnp.bfloat16


def _bf(x):
    return x.astype(BF16)


def _dot(a, b):
    return jnp.dot(a, b, preferred_element_type=F32)


def _split(x, parts):
    out = []
    rem = x
    for _ in range(parts):
        p = rem.astype(BF16)
        out.append(p)
        rem = rem - p.astype(F32)
    return out


def _head_sums(x, e_pair, parts=2):
    cols = []
    for j in range(x.shape[1] // PAIR):
        xs = x[:, j * PAIR:(j + 1) * PAIR]
        cols.append(sum(_dot(p, e_pair) for p in _split(xs, parts)))
    return jnp.concatenate(cols, axis=1)


def _sigmoid(x):
    return 1.0 / (1.0 + jnp.exp(-x))


def _rms(x, w):
    ms = jnp.mean(x * x, axis=-1, keepdims=True)
    return x * lax.rsqrt(ms + NORM_EPS) * w


def _const_spec(shape):
    nd = len(shape)
    return pl.BlockSpec(shape, lambda *_: (0,) * nd, pipeline_mode=pl.Buffered(1))


def _params(sem):
    return pltpu.CompilerParams(dimension_semantics=sem, vmem_limit_bytes=VMEM_LIMIT)


def _mod_kernel(c_ref, w_ref, b_ref, o_ref):
    c = c_ref[...]
    s = c * _sigmoid(c)
    o_ref[...] = _dot(_bf(s), _bf(w_ref[...])) + b_ref[...]


def _modulation(cc, ada_w, ada_b):
    rows, d = cc.shape
    n = ada_w.shape[1]
    tn = n // 8
    return pl.pallas_call(
        _mod_kernel,
        grid=(n // tn,),
        in_specs=[pl.BlockSpec((rows, d), lambda j: (0, 0)),
                  pl.BlockSpec((d, tn), lambda j: (0, j)),
                  pl.BlockSpec((1, tn), lambda j: (0, j))],
        out_specs=pl.BlockSpec((rows, tn), lambda j: (0, j)),
        out_shape=jax.ShapeDtypeStruct((rows, n), F32),
        compiler_params=_params(("arbitrary",)),
        name="adaln_mod",
    )(cc, ada_w, ada_b.reshape(1, n))


def _ffn_kernel(x_ref, mod_ref, nw_ref, w13_ref, w2_ref, fn_ref, o_ref, *, mod_off, norm_row, final):
    x = x_ref[0]
    m = mod_ref[0]
    shift = m[mod_off:mod_off + 1]
    scale = m[mod_off + 1:mod_off + 2]
    gate = m[mod_off + 2:mod_off + 3]
    h = _rms(x, nw_ref[norm_row:norm_row + 1]) * (1.0 + scale) + shift
    hb = _bf(h)
    acc = jnp.zeros(x.shape, F32)
    for j in range(D_FF // FF_CHUNK):
        a = _dot(hb, w13_ref[:, j * FF_CHUNK:(j + 1) * FF_CHUNK])
        b = _dot(hb, w13_ref[:, D_FF + j * FF_CHUNK:D_FF + (j + 1) * FF_CHUNK])
        g = (a * _sigmoid(a)) * b
        acc = acc + _dot(_bf(g), w2_ref[j * FF_CHUNK:(j + 1) * FF_CHUNK, :])
    y = x + 0.5 * gate * acc
    if final:
        y = _rms(y, fn_ref[...])
    o_ref[0] = y


def _ffn(x, mod, per_batch_mod, norm_w, w13, w2, final_norm, *, mod_off, norm_row, final, tm):
    b, t, d = x.shape
    mod_map = (lambda bi, i: (bi + 1, 0, 0)) if per_batch_mod else (lambda bi, i: (0, 0, 0))
    kern = functools.partial(_ffn_kernel, mod_off=mod_off, norm_row=norm_row, final=final)
    return pl.pallas_call(
        kern,
        grid=(b, t // tm),
        in_specs=[pl.BlockSpec((1, tm, d), lambda bi, i: (bi, i, 0)),
                  pl.BlockSpec((1, N_MOD, d), mod_map),
                  _const_spec(norm_w.shape),
                  _const_spec(w13.shape),
                  _const_spec(w2.shape),
                  _const_spec((1, d))],
        out_specs=pl.BlockSpec((1, tm, d), lambda bi, i: (bi, i, 0)),
        out_shape=jax.ShapeDtypeStruct(x.shape, F32),
        compiler_params=_params(("arbitrary", "arbitrary")),
        name="ffn_final" if final else "ffn",
    )(x, mod, norm_w, w13, w2, final_norm.reshape(1, d))


def _inproj_kernel(x_ref, mod_ref, nw_ref, w_ref, rkv_ref, lora_ref, f_ref, gate_ref):
    x = x_ref[0]
    m = mod_ref[0]
    h = _rms(x, nw_ref[1:2]) * (1.0 + m[4:5]) + m[3:4]
    hb = _bf(h)
    rkv_ref[0] = _dot(hb, w_ref[:, :OFF_WD])
    lora_ref[0] = _dot(hb, w_ref[:, OFF_WD:OFF_F])
    f_ref[0] = _dot(hb, w_ref[:, OFF_F:OFF_GATE])
    gate_ref[0] = _dot(hb, w_ref[:, OFF_GATE:])


def _inproj(x, mod, per_batch_mod, norm_w, w_in, *, tm):
    b, t, d = x.shape
    mod_map = (lambda bi, i: (bi + 1, 0, 0)) if per_batch_mod else (lambda bi, i: (0, 0, 0))
    widths = (OFF_WD, LORA_W, FOURIER_WIDTH, N_BRANCHES * D_MODEL)
    return pl.pallas_call(
        _inproj_kernel,
        grid=(b, t // tm),
        in_specs=[pl.BlockSpec((1, tm, d), lambda bi, i: (bi, i, 0)),
                  pl.BlockSpec((1, N_MOD, d), mod_map),
                  _const_spec(norm_w.shape),
                  _const_spec(w_in.shape)],
        out_specs=[pl.BlockSpec((1, tm, w), lambda bi, i: (bi, i, 0)) for w in widths],
        out_shape=[jax.ShapeDtypeStruct((b, t, w), F32) for w in widths],
        compiler_params=_params(("arbitrary", "arbitrary")),
        name="mixer_inproj",
    )(x, mod, norm_w, w_in)


def _bdot(a, b):
    return lax.dot_general(a, b, (((2,), (1,)), ((0,), (0,))), preferred_element_type=F32)


def _bdot_nt(a, b):
    return lax.dot_general(a, b, (((2,), (2,)), ((0,), (0,))), preferred_element_type=F32)


def _tri_inverse(ab, same_blk, eye_f, bd):
    c = ab.shape[1]
    n1 = jnp.where(same_blk[SUBLANES], ab, 0.0)
    n2 = _bdot(_bf(n1), bd(n1))
    n34 = _bdot(_bf(jnp.concatenate([n1, n2], axis=1)), bd(n2))
    t = eye_f - n1 + n2 - n34[:, :c]
    t = t + _bdot(_bf(t), bd(n34[:, c:]))
    k = SUBLANES
    while k < c:
        off = jnp.where(same_blk[2 * k] & jnp.logical_not(same_blk[k]), ab, 0.0)
        t = t - _bdot(_bf(t), bd(_bdot(_bf(off), bd(t))))
        k *= 2
    return t


def _pairs(x):
    return jnp.stack([x[k * CHUNK:(k + 1) * CHUNK, j * PAIR:(j + 1) * PAIR]
                      for k in range(x.shape[0] // CHUNK) for j in range(N_PAIRS)], axis=0)


def _block_operands(bi, n_blocks, reverse, d, rkv_ref, prev_ref, next_ref, lora_ref, prm, bonus_ref):
    c = CHUNK
    rw = RWKV_WIDTH
    u = rkv_ref[0]
    rows = u.shape[0]
    row = lax.broadcasted_iota(jnp.int32, (rows, 1), 0)
    prev_row = jnp.where(bi > 0, prev_ref[0][SUBLANES - 1:SUBLANES, :], 0.0)
    next_row = jnp.where(bi < n_blocks - 1, next_ref[0][0:1, :], 0.0)
    u_prev = jnp.where(row == 0, prev_row, pltpu.roll(u, 1, 0))
    u_next = jnp.where(row == rows - 1, next_row, pltpu.roll(u, rows - 1, 0))
    cw = prm["conv"][...]
    rkv = u_prev * cw[0:1] + u * cw[1:2] + u_next * cw[2:3]
    r = rkv[:, :rw]
    k = rkv[:, rw:2 * rw]
    v = rkv[:, 2 * rw:]

    lo = lora_ref[0]
    wd = _bf(jnp.tanh(lo[:, :N_DIRS * DECAY_LORA]))
    ad = _bf(lo[:, N_DIRS * DECAY_LORA:N_DIRS * (DECAY_LORA + ICLR_LORA)])
    dsl = slice(d * rw, (d + 1) * rw)
    zw = prm["w0"][d:d + 1] + _dot(wd, prm["wup"][:, dsl])
    w_log = jnp.minimum(zw, 0.0) - jnp.log(1.0 + jnp.exp(-jnp.abs(zw))) - 0.5
    lw = -jnp.exp(w_log)
    a = _sigmoid(prm["a0"][d:d + 1] + _dot(ad, prm["aup"][:, dsl]))
    k_a = prm["k_a"][...]
    eseg = prm["eseg"][...]
    kk = k * prm["k_k"][...]
    ss = _head_sums(kk * kk, eseg)
    kk = kk / jnp.maximum(jnp.sqrt(ss), 1e-12)
    kd = k * (1.0 + (a - 1.0) * k_a)
    b = kk * a

    if bonus_ref is not None:
        osl = slice((1 - d) * rw, (2 - d) * rw)
        a_o = _sigmoid(prm["a0"][1 - d:2 - d] + _dot(ad, prm["aup"][:, osl]))
        kd_o = k * (1.0 + (a_o - 1.0) * k_a)
        rk = _head_sums(r * prm["r_k"][...] * (kd + kd_o), eseg)
        bonus_ref[0] = rk * v

    ti = lax.broadcasted_iota(jnp.int32, (rows, rows), 0)
    si = lax.broadcasted_iota(jnp.int32, (rows, rows), 1)
    earlier = (si >= ti) if reverse else (si <= ti)
    tri = _bf((earlier & ((ti // c) == (si // c))).astype(F32))
    incl = sum(_dot(tri, p) for p in _split(lw, 3))
    last = 0 if reverse else c - 1
    ends = [incl[j * c + last:j * c + last + 1] for j in range(rows // c)]
    total = jnp.concatenate([jnp.broadcast_to(e, (c, rw)) for e in ends], axis=0)
    p_end = jnp.concatenate([jnp.broadcast_to(jnp.exp(e), (c, rw)) for e in ends], axis=0)
    e_ninc = jnp.exp(-incl)
    e_tot = jnp.exp(total - incl)
    return dict(kkm=_pairs(kk * jnp.exp(incl - lw)), rp=_pairs(r * jnp.exp(incl)),
                km=_pairs(kd * e_ninc), bm=_pairs(b * e_ninc), khat=_pairs(kd * e_tot),
                bhat=_pairs(b * e_tot), v=_pairs(v), p_end=_pairs(p_end))


def _scan_kernel(rkv_f, prev_f, next_f, lora_f, rkv_b, prev_b, next_b, lora_b,
                 conv_ref, w0_ref, wup_ref, a0_ref, aup_ref, kk_ref, ka_ref, rk_ref, eseg_ref,
                 hf0_ref, hb0_ref, of_ref, ob_ref, bonus_ref, hf_ref, hb_ref, *, n_blocks):
    i = pl.program_id(1)
    c = CHUNK
    n = HEAD_DIM
    npr = N_PAIRS
    nsub = SUB_CHUNKS
    half = nsub * npr
    p2 = N_DIRS * half

    @pl.when(i == 0)
    def _():
        hf_ref[...] = hf0_ref[...]
        hb_ref[...] = hb0_ref[...]

    prm = dict(conv=conv_ref, w0=w0_ref, wup=wup_ref, a0=a0_ref, aup=aup_ref, k_k=kk_ref,
               k_a=ka_ref, r_k=rk_ref, eseg=eseg_ref)
    opf = _block_operands(i, n_blocks, False, 0, rkv_f, prev_f, next_f, lora_f, prm, bonus_ref)
    opb = _block_operands(n_blocks - 1 - i, n_blocks, True, 1, rkv_b, prev_b, next_b, lora_b, prm, None)
    op = {key: jnp.concatenate([opf[key], opb[key]], axis=0) for key in opf}

    shape = (p2, c, PAIR)
    rev = lax.broadcasted_iota(jnp.int32, shape, 0) >= half
    ti = lax.broadcasted_iota(jnp.int32, shape, 1)
    lane = lax.broadcasted_iota(jnp.int32, shape, 2)
    lane_hi = lane >= n
    si = jnp.where(lane_hi, lane - n, lane)
    age = jnp.where(rev, ti - si, si - ti)
    strict_m = age < 0
    incl_m = age <= 0
    eye = age == 0
    same_blk = {}
    kb = SUBLANES
    while kb <= c:
        same_blk[kb] = (ti // kb) == (si // kb)
        kb *= 2

    lane_hi_step = lax.broadcasted_iota(jnp.int32, (N_DIRS * npr, c, PAIR), 2) >= n

    def bd(y):
        hi = lane_hi if y.shape[0] == p2 else lane_hi_step
        return _bf(jnp.concatenate([jnp.where(hi, 0.0, y), jnp.where(hi, y, 0.0)], axis=1))

    def diag_blocks(full):
        return jnp.where(lane_hi, full[:, n:], full[:, :n])

    kkm = op["kkm"]
    rp = op["rp"]
    x2 = _bf(jnp.concatenate([kkm, rp], axis=1))
    ab2 = _bdot_nt(x2, bd(op["bm"]))
    ak2 = _bdot_nt(x2, bd(op["km"]))
    a_b = jnp.where(strict_m, ab2[:, :c], 0.0)
    a_rb = _bf(jnp.where(incl_m, ab2[:, c:], 0.0))
    a_k = _bf(jnp.where(strict_m, ak2[:, :c], 0.0))
    a_rk = _bf(jnp.where(incl_m, ak2[:, c:], 0.0))
    tinv = _bf(_tri_inverse(a_b, same_blk, eye.astype(F32), bd))
    v_bd = bd(op["v"])
    av = _bdot(jnp.concatenate([a_k, a_rk], axis=1), v_bd)
    akv = av[:, :c]
    wt = _bdot(tinv, bd(kkm))
    ut = _bdot(tinv, bd(akv))
    qt = _bf(rp - _bdot(a_rb, bd(wt)))
    ot = av[:, c:] - _bdot(a_rb, bd(ut))
    bhat_t = _bf(jnp.swapaxes(op["bhat"], 1, 2))
    khat_t = _bf(jnp.swapaxes(op["khat"], 1, 2))
    mc = _bf(diag_blocks(_bdot(bhat_t, _bf(wt))))
    g = diag_blocks(_bdot(khat_t, _bf(op["v"])) - _bdot(bhat_t, _bf(ut)))
    dec = jnp.where(eye, op["p_end"], 0.0)
    p_lo = jnp.sum(jnp.where(lane_hi, 0.0, dec), axis=2, keepdims=True)
    p_hi = jnp.sum(jnp.where(lane_hi, dec, 0.0), axis=2, keepdims=True)
    p_col = jnp.where(lane_hi, p_hi, p_lo)
    qm = jnp.concatenate([qt, mc], axis=1)

    hh = jnp.concatenate([hf_ref[0], hb_ref[0]], axis=0)
    for step in range(nsub):
        kf = step
        kr = nsub - 1 - step

        def ent(x, kf=kf, kr=kr):
            return jnp.concatenate([x[kf * npr:(kf + 1) * npr],
                                    x[half + kr * npr:half + (kr + 1) * npr]], axis=0)

        qmh = _bdot(ent(qm), bd(hh))
        o = qmh[:, :c] + ent(ot)
        hh = ent(p_col) * hh - qmh[:, c:] + ent(g)
        for j in range(npr):
            ps = slice(j * PAIR, (j + 1) * PAIR)
            of_ref[0, kf * c:(kf + 1) * c, ps] = o[j]
            ob_ref[0, kr * c:(kr + 1) * c, ps] = o[npr + j]
    hf_ref[0] = hh[:npr]
    hb_ref[0] = hh[npr:]


def _scan(u_rkv, u_lora, hf0, hb0, conv, w0, wup_bd, a0, aup_bd, k_k, k_a, r_k, eseg):
    b, t, _ = u_rkv.shape
    blk = SUB_CHUNKS * CHUNK
    n = t // blk
    per8 = blk // SUBLANES
    last8 = t // SUBLANES - 1
    rw = RWKV_WIDTH

    def fwd(bi, i):
        return (bi, i, 0)

    def fwd_prev(bi, i):
        return (bi, jnp.maximum(i * per8 - 1, 0), 0)

    def fwd_next(bi, i):
        return (bi, jnp.minimum((i + 1) * per8, last8), 0)

    def bwd(bi, i):
        return (bi, n - 1 - i, 0)

    def bwd_prev(bi, i):
        return (bi, jnp.maximum((n - 1 - i) * per8 - 1, 0), 0)

    def bwd_next(bi, i):
        return (bi, jnp.minimum((n - i) * per8, last8), 0)

    def seq_specs(main, prev, nxt):
        return [pl.BlockSpec((1, blk, 3 * rw), main),
                pl.BlockSpec((1, SUBLANES, 3 * rw), prev),
                pl.BlockSpec((1, SUBLANES, 3 * rw), nxt),
                pl.BlockSpec((1, blk, LORA_W), main)]

    state_spec = pl.BlockSpec((1, N_PAIRS, HEAD_DIM, PAIR), lambda bi, i: (bi, 0, 0, 0))
    consts = (conv, w0, wup_bd, a0, aup_bd, k_k, k_a, r_k, eseg)
    seq_shape = jax.ShapeDtypeStruct((b, t, rw), F32)
    return pl.pallas_call(
        functools.partial(_scan_kernel, n_blocks=n),
        grid=(b, n),
        in_specs=(seq_specs(fwd, fwd_prev, fwd_next) + seq_specs(bwd, bwd_prev, bwd_next)
                  + [_const_spec(x.shape) for x in consts] + [state_spec, state_spec]),
        out_specs=[pl.BlockSpec((1, blk, rw), fwd), pl.BlockSpec((1, blk, rw), bwd),
                   pl.BlockSpec((1, blk, rw), fwd), state_spec, state_spec],
        out_shape=[seq_shape, seq_shape, seq_shape,
                   jax.ShapeDtypeStruct(hf0.shape, F32), jax.ShapeDtypeStruct(hb0.shape, F32)],
        compiler_params=_params(("arbitrary", "arbitrary")),
        name="rwkv7_scan",
    )(u_rkv, u_rkv, u_rkv, u_lora, u_rkv, u_rkv, u_rkv, u_lora, *consts, hf0, hb0)


def _dft_tables(n):
    ang = 2.0 * np.pi * (np.outer(np.arange(n), np.arange(n)) % n) / n
    return np.cos(ang), np.sin(ang)


def _table(x):
    return jnp.asarray(x, F32).astype(BF16)


def _four_ctx_kernel(f_ref, wc_ref, ct_ref, st_ref, o_ref):
    y = _bf(_dot(_bf(f_ref[0]), wc_ref[...]))
    w = FOURIER_WIDTH
    o_ref[0] = _dot(ct_ref[...], y[:, :w]) - _dot(st_ref[...], y[:, w:])


def _fourier_ctx(f):
    b, t, w = f.shape
    gd = FOURIER_GROUP_DIM
    cc, sc = _dft_tables(gd)
    wc = np.zeros((w, 2 * w), np.float32)
    for g in range(FOURIER_GROUPS):
        wc[g * gd:(g + 1) * gd, g * gd:(g + 1) * gd] = cc
        wc[g * gd:(g + 1) * gd, w + g * gd:w + (g + 1) * gd] = sc
    ct, st = _dft_tables(t)
    norm = 1.0 / np.sqrt(t * gd)
    consts = [_table(wc), _table(ct * norm), _table(st * norm)]
    return pl.pallas_call(
        _four_ctx_kernel,
        grid=(b,),
        in_specs=[pl.BlockSpec((1, t, w), lambda bi: (bi, 0, 0))] + [_const_spec(x.shape) for x in consts],
        out_specs=pl.BlockSpec((1, t, w), lambda bi: (bi, 0, 0)),
        out_shape=jax.ShapeDtypeStruct(f.shape, F32),
        compiler_params=_params(("arbitrary",)),
        name="fourier_ctx",
    )(f, *consts)


W_ROWS = 4


def _four_lat_kernel(f_ref, wc_ref, kcw_ref, ksw_ref, kcr_ref, ksr_ref, o_ref, z_ref):
    gd = FOURIER_GROUP_DIM
    rows = f_ref.shape[1]
    x = _bf(f_ref[0].reshape(rows * GRID_W, gd))
    y = _bf(_dot(x, wc_ref[...]))
    kcw = kcw_ref[...]
    ksw = ksw_ref[...]
    blk = W_ROWS * GRID_W
    for j in range(rows // W_ROWS):
        yy = y[j * blk:(j + 1) * blk]
        pc = _dot(kcw, yy)
        ps = _dot(ksw, yy)
        z_ref[j * W_ROWS:(j + 1) * W_ROWS, :, :gd] = (pc[:, :gd] - ps[:, gd:]).reshape(W_ROWS, GRID_W, gd)
        z_ref[j * W_ROWS:(j + 1) * W_ROWS, :, gd:] = (ps[:, :gd] + pc[:, gd:]).reshape(W_ROWS, GRID_W, gd)
    kcr = kcr_ref[...]
    ksr = ksr_ref[...]
    for j in range(GRID_W // SUBLANES):
        ws = slice(j * SUBLANES, (j + 1) * SUBLANES)
        zg = _bf(z_ref[:, ws, :].reshape(rows * SUBLANES, 2 * gd))
        re = _dot(kcr, zg)[:, :gd] - _dot(ksr, zg)[:, gd:]
        o_ref[0, :, ws, :] = re.reshape(rows, SUBLANES, gd)


def _fourier_lat(f):
    b, t, w = f.shape
    gd = FOURIER_GROUP_DIM
    rows = t // GRID_W
    cc, sc = _dft_tables(gd)
    cw, sw = _dft_tables(GRID_W)
    cr, sr = _dft_tables(rows)
    norm = 1.0 / np.sqrt(rows * GRID_W * gd)
    eye_w = np.eye(W_ROWS)
    eye_8 = np.eye(SUBLANES)
    consts = [_table(np.concatenate([cc, sc], axis=1)),
              _table(np.kron(eye_w, cw)), _table(np.kron(eye_w, sw)),
              _table(np.kron(cr, eye_8) * norm), _table(np.kron(sr, eye_8) * norm)]
    f4 = f.reshape(b, rows, GRID_W, w)
    blk = pl.BlockSpec((1, rows, GRID_W, gd), lambda bi, g: (bi, 0, 0, g))
    out = pl.pallas_call(
        _four_lat_kernel,
        grid=(b, FOURIER_GROUPS),
        in_specs=[blk] + [_const_spec(x.shape) for x in consts],
        out_specs=blk,
        out_shape=jax.ShapeDtypeStruct(f4.shape, F32),
        scratch_shapes=[pltpu.VMEM((rows, GRID_W, 2 * gd), F32)],
        compiler_params=_params(("arbitrary", "arbitrary")),
        name="fourier_lat",
    )(f4, *consts)
    return out.reshape(b, t, w)


def _mixout_kernel(x_ref, mod_ref, of_ref, ob_ref, bonus_ref, lora_ref, four_ref, gate_ref,
                   eseg_ref, lnw_ref, lnb_ref, gup_ref, gb_ref, wo_ref, wf_ref, wout_ref, o_ref):
    eseg = eseg_ref[...]
    inv_n = 1.0 / HEAD_DIM
    o = of_ref[0] + ob_ref[0]
    mu = _head_sums(o, eseg) * inv_n
    dlt = o - mu
    var = _head_sums(dlt * dlt, eseg) * inv_n
    on = dlt * lax.rsqrt(var + GN_EPS) * lnw_ref[...] + lnb_ref[...]
    gd = lora_ref[0][:, N_DIRS * (DECAY_LORA + ICLR_LORA):]
    g = _dot(_bf(_sigmoid(gd)), gup_ref[...])
    y_rwkv = _dot(_bf((on + bonus_ref[0]) * g), wo_ref[...])
    y_four = _dot(_bf(four_ref[0]), wf_ref[...])
    gp = gate_ref[0]
    gb = gb_ref[...]
    g0 = _sigmoid(gp[:, :D_MODEL] + gb[0:1])
    g1 = _sigmoid(gp[:, D_MODEL:] + gb[1:2])
    merged = g0 * y_rwkv + g1 * y_four
    y = _dot(_bf(merged), wout_ref[...])
    o_ref[0] = x_ref[0] + mod_ref[0][5:6] * y


def _mixout(x, mod, per_batch_mod, o_f, o_b, bonus, u_lora, four, gate_pre, consts, *, tm):
    b, t, d = x.shape
    mod_map = (lambda bi, i: (bi + 1, 0, 0)) if per_batch_mod else (lambda bi, i: (0, 0, 0))

    def seq(w):
        return pl.BlockSpec((1, tm, w), lambda bi, i: (bi, i, 0))

    return pl.pallas_call(
        _mixout_kernel,
        grid=(b, t // tm),
        in_specs=([seq(d), pl.BlockSpec((1, N_MOD, d), mod_map), seq(RWKV_WIDTH), seq(RWKV_WIDTH),
                   seq(RWKV_WIDTH), seq(LORA_W), seq(FOURIER_WIDTH), seq(N_BRANCHES * D_MODEL)]
                  + [_const_spec(c.shape) for c in consts]),
        out_specs=seq(d),
        out_shape=jax.ShapeDtypeStruct(x.shape, F32),
        compiler_params=_params(("arbitrary", "arbitrary")),
        name="mixer_out",
    )(x, mod, o_f, o_b, bonus, u_lora, four, gate_pre, *consts)


def _state_to_pairs(s):
    b = s.shape[0]
    s = s.reshape(b, N_PAIRS, 2, HEAD_DIM, HEAD_DIM)
    return jnp.transpose(s, (0, 1, 4, 2, 3)).reshape(b, N_PAIRS, HEAD_DIM, PAIR)


def _pairs_to_state(hp):
    b = hp.shape[0]
    hp = hp.reshape(b, N_PAIRS, HEAD_DIM, 2, HEAD_DIM)
    return jnp.transpose(hp, (0, 1, 3, 4, 2)).reshape(b, RWKV_HEADS, HEAD_DIM, HEAD_DIM)


def _block_diag_lora(w_up):
    dirs, rank, width = w_up.shape
    out = jnp.zeros((dirs * rank, dirs * width), F32)
    for d in range(dirs):
        out = out.at[d * rank:(d + 1) * rank, d * width:(d + 1) * width].set(w_up[d])
    return _bf(out)


def kernel(x_prompt, x_sample, state_rwkv_fwd, state_rwkv_bwd, c, c_ctx, ada_w, ada_b, norm_w,
           ffn1_w13, ffn1_w2, ffn2_w13, ffn2_w2, w_in, gate_b, conv_rkv, w0, w_up, a0, a_up, g_up,
           k_k, k_a, r_k, lnx_w, lnx_b, w_o_rwkv, w_fourier, w_out, final_norm):
    depth = ada_w.shape[0]
    b_p = x_prompt.shape[0]
    b_s = x_sample.shape[0]
    rw = RWKV_WIDTH
    seg = np.arange(PAIR) // HEAD_DIM
    eseg = jnp.asarray(seg[:, None] == seg[None, :], BF16)
    cc = jnp.zeros((2 * SUBLANES, D_MODEL), F32).at[0].set(c_ctx).at[1:1 + b_s].set(c)
    zero_state = jnp.zeros((b_p, N_PAIRS, HEAD_DIM, PAIR), F32)

    x_p, x_s = x_prompt, x_sample
    new_f, new_b = [], []
    for l in range(depth):
        mod = _modulation(cc, ada_w[l], ada_b[l])[:1 + b_s].reshape(1 + b_s, N_MOD, D_MODEL)
        w13_1, w2_1 = _bf(ffn1_w13[l]), _bf(ffn1_w2[l])
        w13_2, w2_2 = _bf(ffn2_w13[l]), _bf(ffn2_w2[l])
        w_in_b = _bf(w_in[l])
        scan_consts = (conv_rkv[l], w0[l], _block_diag_lora(w_up[l]), a0[l], _block_diag_lora(a_up[l]),
                       k_k[l].reshape(1, rw), k_a[l].reshape(1, rw), r_k[l].reshape(1, rw), eseg)
        mix_consts = (eseg, lnx_w[l].reshape(1, rw), lnx_b[l].reshape(1, rw), _bf(g_up[l]), gate_b[l],
                      _bf(w_o_rwkv[l]), _bf(w_fourier[l]), _bf(w_out[l]))
        last = l == depth - 1

        def layer(x, per_batch_mod, hf0, hb0, grid, tm):
            x = _ffn(x, mod, per_batch_mod, norm_w[l], w13_1, w2_1, final_norm,
                     mod_off=0, norm_row=0, final=False, tm=tm)
            u_rkv, u_lora, f, gate_pre = _inproj(x, mod, per_batch_mod, norm_w[l], w_in_b, tm=tm)
            o_f, o_b, bonus, hf, hb = _scan(u_rkv, u_lora, hf0, hb0, *scan_consts)
            four = _fourier_lat(f) if grid else _fourier_ctx(f)
            x = _mixout(x, mod, per_batch_mod, o_f, o_b, bonus, u_lora, four, gate_pre, mix_consts, tm=tm)
            x = _ffn(x, mod, per_batch_mod, norm_w[l], w13_2, w2_2, final_norm,
                     mod_off=6, norm_row=2, final=last, tm=tm)
            return x, hf, hb

        x_p, hf, hb = layer(x_p, False, zero_state, zero_state, False, 256)
        new_f.append(_pairs_to_state(hf))
        new_b.append(_pairs_to_state(hb))
        hf0 = _state_to_pairs(state_rwkv_fwd[:, l])
        hb0 = _state_to_pairs(state_rwkv_bwd[:, l])
        x_s, _, _ = layer(x_s, True, hf0, hb0, True, 512)
    if depth == 0:
        raise ValueError("depth must be positive")
    return (x_p, x_s, jnp.stack(new_f, axis=1), jnp.stack(new_b, axis=1))
```

```python
import functools

import numpy as np
import jax
import jax.numpy as jnp
from jax import lax
from jax.experimental import pallas as pl
from jax.experimental.pallas import tpu as pltpu

D_MODEL = 1024
RWKV_HEADS = 8
HEAD_DIM = 64
RWKV_WIDTH = RWKV_HEADS * HEAD_DIM
DECAY_LORA = 64
ICLR_LORA = 64
GATE_LORA = 128
N_DIRS = 2
FOURIER_GROUPS = 4
FOURIER_GROUP_DIM = 128
FOURIER_WIDTH = FOURIER_GROUPS * FOURIER_GROUP_DIM
GRID_W = 64
N_BRANCHES = 2
D_FF = 2816
N_MOD = 9
NORM_EPS = 1e-6
GN_EPS = 64e-5

OFF_WD = 3 * RWKV_WIDTH
LORA_W = N_DIRS * DECAY_LORA + N_DIRS * ICLR_LORA + GATE_LORA
OFF_F = OFF_WD + LORA_W
OFF_GATE = OFF_F + FOURIER_WIDTH
PROJ_W = OFF_GATE + N_BRANCHES * D_MODEL

CHUNK = 64
PAIR = 2 * HEAD_DIM
N_PAIRS = RWKV_HEADS // 2
SUB_CHUNKS = 2
SUBLANES = 8
FF_CHUNK = 256
VMEM_LIMIT = 56 * 1024 * 1024

F32 = jnp.float32
BF16 = jnp.bfloat16


def _bf(x):
    return x.astype(BF16)


def _dot(a, b):
    return jnp.dot(a, b, preferred_element_type=F32)


def _split(x, parts):
    out = []
    rem = x
    for _ in range(parts):
        p = rem.astype(BF16)
        out.append(p)
        rem = rem - p.astype(F32)
    return out


def _head_sums(x, e_pair, parts=2):
    cols = []
    for j in range(x.shape[1] // PAIR):
        xs = x[:, j * PAIR:(j + 1) * PAIR]
        cols.append(sum(_dot(p, e_pair) for p in _split(xs, parts)))
    return jnp.concatenate(cols, axis=1)


def _sigmoid(x):
    return 1.0 / (1.0 + jnp.exp(-x))


def _rms(x, w):
    ms = jnp.mean(x * x, axis=-1, keepdims=True)
    return x * lax.rsqrt(ms + NORM_EPS) * w


def _const_spec(shape):
    nd = len(shape)
    return pl.BlockSpec(shape, lambda *_: (0,) * nd, pipeline_mode=pl.Buffered(1))


def _params(sem):
    return pltpu.CompilerParams(dimension_semantics=sem, vmem_limit_bytes=VMEM_LIMIT)


def _mod_kernel(c_ref, w_ref, b_ref, o_ref):
    c = c_ref[...]
    s = c * _sigmoid(c)
    o_ref[...] = _dot(_bf(s), _bf(w_ref[...])) + b_ref[...]


def _modulation(cc, ada_w, ada_b):
    rows, d = cc.shape
    n = ada_w.shape[1]
    tn = n // 8
    return pl.pallas_call(
        _mod_kernel,
        grid=(n // tn,),
        in_specs=[pl.BlockSpec((rows, d), lambda j: (0, 0)),
                  pl.BlockSpec((d, tn), lambda j: (0, j)),
                  pl.BlockSpec((1, tn), lambda j: (0, j))],
        out_specs=pl.BlockSpec((rows, tn), lambda j: (0, j)),
        out_shape=jax.ShapeDtypeStruct((rows, n), F32),
        compiler_params=_params(("arbitrary",)),
        name="adaln_mod",
    )(cc, ada_w, ada_b.reshape(1, n))


def _ffn_kernel(x_ref, mod_ref, nw_ref, w13_ref, w2_ref, fn_ref, o_ref, *, mod_off, norm_row, final):
    x = x_ref[0]
    m = mod_ref[0]
    shift = m[mod_off:mod_off + 1]
    scale = m[mod_off + 1:mod_off + 2]
    gate = m[mod_off + 2:mod_off + 3]
    h = _rms(x, nw_ref[norm_row:norm_row + 1]) * (1.0 + scale) + shift
    hb = _bf(h)
    acc = jnp.zeros(x.shape, F32)
    for j in range(D_FF // FF_CHUNK):
        a = _dot(hb, w13_ref[:, j * FF_CHUNK:(j + 1) * FF_CHUNK])
        b = _dot(hb, w13_ref[:, D_FF + j * FF_CHUNK:D_FF + (j + 1) * FF_CHUNK])
        g = (a * _sigmoid(a)) * b
        acc = acc + _dot(_bf(g), w2_ref[j * FF_CHUNK:(j + 1) * FF_CHUNK, :])
    y = x + 0.5 * gate * acc
    if final:
        y = _rms(y, fn_ref[...])
    o_ref[0] = y


def _ffn(x, mod, per_batch_mod, norm_w, w13, w2, final_norm, *, mod_off, norm_row, final, tm):
    b, t, d = x.shape
    mod_map = (lambda bi, i: (bi + 1, 0, 0)) if per_batch_mod else (lambda bi, i: (0, 0, 0))
    kern = functools.partial(_ffn_kernel, mod_off=mod_off, norm_row=norm_row, final=final)
    return pl.pallas_call(
        kern,
        grid=(b, t // tm),
        in_specs=[pl.BlockSpec((1, tm, d), lambda bi, i: (bi, i, 0)),
                  pl.BlockSpec((1, N_MOD, d), mod_map),
                  _const_spec(norm_w.shape),
                  _const_spec(w13.shape),
                  _const_spec(w2.shape),
                  _const_spec((1, d))],
        out_specs=pl.BlockSpec((1, tm, d), lambda bi, i: (bi, i, 0)),
        out_shape=jax.ShapeDtypeStruct(x.shape, F32),
        compiler_params=_params(("arbitrary", "arbitrary")),
        name="ffn_final" if final else "ffn",
    )(x, mod, norm_w, w13, w2, final_norm.reshape(1, d))


def _inproj_kernel(x_ref, mod_ref, nw_ref, w_ref, rkv_ref, lora_ref, f_ref, gate_ref):
    x = x_ref[0]
    m = mod_ref[0]
    h = _rms(x, nw_ref[1:2]) * (1.0 + m[4:5]) + m[3:4]
    hb = _bf(h)
    rkv_ref[0] = _dot(hb, w_ref[:, :OFF_WD])
    lora_ref[0] = _dot(hb, w_ref[:, OFF_WD:OFF_F])
    f_ref[0] = _dot(hb, w_ref[:, OFF_F:OFF_GATE])
    gate_ref[0] = _dot(hb, w_ref[:, OFF_GATE:])


def _inproj(x, mod, per_batch_mod, norm_w, w_in, *, tm):
    b, t, d = x.shape
    mod_map = (lambda bi, i: (bi + 1, 0, 0)) if per_batch_mod else (lambda bi, i: (0, 0, 0))
    widths = (OFF_WD, LORA_W, FOURIER_WIDTH, N_BRANCHES * D_MODEL)
    return pl.pallas_call(
        _inproj_kernel,
        grid=(b, t // tm),
        in_specs=[pl.BlockSpec((1, tm, d), lambda bi, i: (bi, i, 0)),
                  pl.BlockSpec((1, N_MOD, d), mod_map),
                  _const_spec(norm_w.shape),
                  _const_spec(w_in.shape)],
        out_specs=[pl.BlockSpec((1, tm, w), lambda bi, i: (bi, i, 0)) for w in widths],
        out_shape=[jax.ShapeDtypeStruct((b, t, w), F32) for w in widths],
        compiler_params=_params(("arbitrary", "arbitrary")),
        name="mixer_inproj",
    )(x, mod, norm_w, w_in)


def _bdot(a, b):
    return lax.dot_general(a, b, (((2,), (1,)), ((0,), (0,))), preferred_element_type=F32)


def _bdot_nt(a, b):
    return lax.dot_general(a, b, (((2,), (2,)), ((0,), (0,))), preferred_element_type=F32)


def _tri_inverse(ab, same_blk, eye_f, bd):
    c = ab.shape[1]
    n1 = jnp.where(same_blk[SUBLANES], ab, 0.0)
    n2 = _bdot(_bf(n1), bd(n1))
    n34 = _bdot(_bf(jnp.concatenate([n1, n2], axis=1)), bd(n2))
    t = eye_f - n1 + n2 - n34[:, :c]
    t = t + _bdot(_bf(t), bd(n34[:, c:]))
    k = SUBLANES
    while k < c:
        off = jnp.where(same_blk[2 * k] & jnp.logical_not(same_blk[k]), ab, 0.0)
        t = t - _bdot(_bf(t), bd(_bdot(_bf(off), bd(t))))
        k *= 2
    return t


def _pairs(x):
    return jnp.stack([x[k * CHUNK:(k + 1) * CHUNK, j * PAIR:(j + 1) * PAIR]
                      for k in range(x.shape[0] // CHUNK) for j in range(N_PAIRS)], axis=0)


def _block_operands(bi, n_blocks, reverse, d, rkv_ref, prev_ref, next_ref, lora_ref, prm, bonus_ref):
    c = CHUNK
    rw = RWKV_WIDTH
    u = rkv_ref[0]
    rows = u.shape[0]
    row = lax.broadcasted_iota(jnp.int32, (rows, 1), 0)
    prev_row = jnp.where(bi > 0, prev_ref[0][SUBLANES - 1:SUBLANES, :], 0.0)
    next_row = jnp.where(bi < n_blocks - 1, next_ref[0][0:1, :], 0.0)
    u_prev = jnp.where(row == 0, prev_row, pltpu.roll(u, 1, 0))
    u_next = jnp.where(row == rows - 1, next_row, pltpu.roll(u, rows - 1, 0))
    cw = prm["conv"][...]
    rkv = u_prev * cw[0:1] + u * cw[1:2] + u_next * cw[2:3]
    r = rkv[:, :rw]
    k = rkv[:, rw:2 * rw]
    v = rkv[:, 2 * rw:]

    lo = lora_ref[0]
    wd = _bf(jnp.tanh(lo[:, :N_DIRS * DECAY_LORA]))
    ad = _bf(lo[:, N_DIRS * DECAY_LORA:N_DIRS * (DECAY_LORA + ICLR_LORA)])
    dsl = slice(d * rw, (d + 1) * rw)
    zw = prm["w0"][d:d + 1] + _dot(wd, prm["wup"][:, dsl])
    w_log = jnp.minimum(zw, 0.0) - jnp.log(1.0 + jnp.exp(-jnp.abs(zw))) - 0.5
    lw = -jnp.exp(w_log)
    a = _sigmoid(prm["a0"][d:d + 1] + _dot(ad, prm["aup"][:, dsl]))
    k_a = prm["k_a"][...]
    eseg = prm["eseg"][...]
    kk = k * prm["k_k"][...]
    ss = _head_sums(kk * kk, eseg)
    kk = kk / jnp.maximum(jnp.sqrt(ss), 1e-12)
    kd = k * (1.0 + (a - 1.0) * k_a)
    b = kk * a

    if bonus_ref is not None:
        osl = slice((1 - d) * rw, (2 - d) * rw)
        a_o = _sigmoid(prm["a0"][1 - d:2 - d] + _dot(ad, prm["aup"][:, osl]))
        kd_o = k * (1.0 + (a_o - 1.0) * k_a)
        rk = _head_sums(r * prm["r_k"][...] * (kd + kd_o), eseg)
        bonus_ref[0] = rk * v

    ti = lax.broadcasted_iota(jnp.int32, (rows, rows), 0)
    si = lax.broadcasted_iota(jnp.int32, (rows, rows), 1)
    earlier = (si >= ti) if reverse else (si <= ti)
    tri = _bf((earlier & ((ti // c) == (si // c))).astype(F32))
    incl = sum(_dot(tri, p) for p in _split(lw, 3))
    last = 0 if reverse else c - 1
    ends = [incl[j * c + last:j * c + last + 1] for j in range(rows // c)]
    total = jnp.concatenate([jnp.broadcast_to(e, (c, rw)) for e in ends], axis=0)
    p_end = jnp.concatenate([jnp.broadcast_to(jnp.exp(e), (c, rw)) for e in ends], axis=0)
    e_ninc = jnp.exp(-incl)
    e_tot = jnp.exp(total - incl)
    return dict(kkm=_pairs(kk * jnp.exp(incl - lw)), rp=_pairs(r * jnp.exp(incl)),
                km=_pairs(kd * e_ninc), bm=_pairs(b * e_ninc), khat=_pairs(kd * e_tot),
                bhat=_pairs(b * e_tot), v=_pairs(v), p_end=_pairs(p_end))


def _scan_kernel(rkv_f, prev_f, next_f, lora_f, rkv_b, prev_b, next_b, lora_b,
                 conv_ref, w0_ref, wup_ref, a0_ref, aup_ref, kk_ref, ka_ref, rk_ref, eseg_ref,
                 hf0_ref, hb0_ref, of_ref, ob_ref, bonus_ref, hf_ref, hb_ref, *, n_blocks):
    i = pl.program_id(1)
    c = CHUNK
    n = HEAD_DIM
    npr = N_PAIRS
    nsub = SUB_CHUNKS
    half = nsub * npr
    p2 = N_DIRS * half

    @pl.when(i == 0)
    def _():
        hf_ref[...] = hf0_ref[...]
        hb_ref[...] = hb0_ref[...]

    prm = dict(conv=conv_ref, w0=w0_ref, wup=wup_ref, a0=a0_ref, aup=aup_ref, k_k=kk_ref,
               k_a=ka_ref, r_k=rk_ref, eseg=eseg_ref)
    opf = _block_operands(i, n_blocks, False, 0, rkv_f, prev_f, next_f, lora_f, prm, bonus_ref)
    opb = _block_operands(n_blocks - 1 - i, n_blocks, True, 1, rkv_b, prev_b, next_b, lora_b, prm, None)
    op = {key: jnp.concatenate([opf[key], opb[key]], axis=0) for key in opf}

    shape = (p2, c, PAIR)
    rev = lax.broadcasted_iota(jnp.int32, shape, 0) >= half
    ti = lax.broadcasted_iota(jnp.int32, shape, 1)
    lane = lax.broadcasted_iota(jnp.int32, shape, 2)
    lane_hi = lane >= n
    si = jnp.where(lane_hi, lane - n, lane)
    age = jnp.where(rev, ti - si, si - ti)
    strict_m = age < 0
    incl_m = age <= 0
    eye = age == 0
    same_blk = {}
    kb = SUBLANES
    while kb <= c:
        same_blk[kb] = (ti // kb) == (si // kb)
        kb *= 2

    lane_hi_step = lax.broadcasted_iota(jnp.int32, (N_DIRS * npr, c, PAIR), 2) >= n

    def bd(y):
        hi = lane_hi if y.shape[0] == p2 else lane_hi_step
        return _bf(jnp.concatenate([jnp.where(hi, 0.0, y), jnp.where(hi, y, 0.0)], axis=1))

    def diag_blocks(full):
        return jnp.where(lane_hi, full[:, n:], full[:, :n])

    kkm = op["kkm"]
    rp = op["rp"]
    x2 = _bf(jnp.concatenate([kkm, rp], axis=1))
    ab2 = _bdot_nt(x2, bd(op["bm"]))
    ak2 = _bdot_nt(x2, bd(op["km"]))
    a_b = jnp.where(strict_m, ab2[:, :c], 0.0)
    a_rb = _bf(jnp.where(incl_m, ab2[:, c:], 0.0))
    a_k = _bf(jnp.where(strict_m, ak2[:, :c], 0.0))
    a_rk = _bf(jnp.where(incl_m, ak2[:, c:], 0.0))
    tinv = _bf(_tri_inverse(a_b, same_blk, eye.astype(F32), bd))
    v_bd = bd(op["v"])
    av = _bdot(jnp.concatenate([a_k, a_rk], axis=1), v_bd)
    akv = av[:, :c]
    wu = _bdot(tinv, jnp.concatenate([bd(kkm), bd(akv)], axis=2))
    wt = wu[:, :, :PAIR]
    ut = wu[:, :, PAIR:]
    arb = _bdot(a_rb, jnp.concatenate([bd(wt), bd(ut)], axis=2))
    qt = _bf(rp - arb[:, :, :PAIR])
    ot = av[:, c:] - arb[:, :, PAIR:]
    bhat_t = _bf(jnp.swapaxes(op["bhat"], 1, 2))
    khat_t = _bf(jnp.swapaxes(op["khat"], 1, 2))
    bwu = _bdot(bhat_t, _bf(wu))
    mc = _bf(diag_blocks(bwu[:, :, :PAIR]))
    g = diag_blocks(_bdot(khat_t, _bf(op["v"])) - bwu[:, :, PAIR:])
    dec = jnp.where(eye, op["p_end"], 0.0)
    p_lo = jnp.sum(jnp.where(lane_hi, 0.0, dec), axis=2, keepdims=True)
    p_hi = jnp.sum(jnp.where(lane_hi, dec, 0.0), axis=2, keepdims=True)
    p_col = jnp.where(lane_hi, p_hi, p_lo)
    qm = jnp.concatenate([qt, mc], axis=1)

    hh = jnp.concatenate([hf_ref[0], hb_ref[0]], axis=0)
    for step in range(nsub):
        kf = step
        kr = nsub - 1 - step

        def ent(x, kf=kf, kr=kr):
            return jnp.concatenate([x[kf * npr:(kf + 1) * npr],
                                    x[half + kr * npr:half + (kr + 1) * npr]], axis=0)

        qmh = _bdot(ent(qm), bd(hh))
        o = qmh[:, :c] + ent(ot)
        hh = ent(p_col) * hh - qmh[:, c:] + ent(g)
        for j in range(npr):
            ps = slice(j * PAIR, (j + 1) * PAIR)
            of_ref[0, kf * c:(kf + 1) * c, ps] = o[j]
            ob_ref[0, kr * c:(kr + 1) * c, ps] = o[npr + j]
    hf_ref[0] = hh[:npr]
    hb_ref[0] = hh[npr:]


def _scan(u_rkv, u_lora, hf0, hb0, conv, w0, wup_bd, a0, aup_bd, k_k, k_a, r_k, eseg):
    b, t, _ = u_rkv.shape
    blk = SUB_CHUNKS * CHUNK
    n = t // blk
    per8 = blk // SUBLANES
    last8 = t // SUBLANES - 1
    rw = RWKV_WIDTH

    def fwd(bi, i):
        return (bi, i, 0)

    def fwd_prev(bi, i):
        return (bi, jnp.maximum(i * per8 - 1, 0), 0)

    def fwd_next(bi, i):
        return (bi, jnp.minimum((i + 1) * per8, last8), 0)

    def bwd(bi, i):
        return (bi, n - 1 - i, 0)

    def bwd_prev(bi, i):
        return (bi, jnp.maximum((n - 1 - i) * per8 - 1, 0), 0)

    def bwd_next(bi, i):
        return (bi, jnp.minimum((n - i) * per8, last8), 0)

    def seq_specs(main, prev, nxt):
        return [pl.BlockSpec((1, blk, 3 * rw), main),
                pl.BlockSpec((1, SUBLANES, 3 * rw), prev),
                pl.BlockSpec((1, SUBLANES, 3 * rw), nxt),
                pl.BlockSpec((1, blk, LORA_W), main)]

    state_spec = pl.BlockSpec((1, N_PAIRS, HEAD_DIM, PAIR), lambda bi, i: (bi, 0, 0, 0))
    consts = (conv, w0, wup_bd, a0, aup_bd, k_k, k_a, r_k, eseg)
    seq_shape = jax.ShapeDtypeStruct((b, t, rw), F32)
    return pl.pallas_call(
        functools.partial(_scan_kernel, n_blocks=n),
        grid=(b, n),
        in_specs=(seq_specs(fwd, fwd_prev, fwd_next) + seq_specs(bwd, bwd_prev, bwd_next)
                  + [_const_spec(x.shape) for x in consts] + [state_spec, state_spec]),
        out_specs=[pl.BlockSpec((1, blk, rw), fwd), pl.BlockSpec((1, blk, rw), bwd),
                   pl.BlockSpec((1, blk, rw), fwd), state_spec, state_spec],
        out_shape=[seq_shape, seq_shape, seq_shape,
                   jax.ShapeDtypeStruct(hf0.shape, F32), jax.ShapeDtypeStruct(hb0.shape, F32)],
        compiler_params=_params(("arbitrary", "arbitrary")),
        name="rwkv7_scan",
    )(u_rkv, u_rkv, u_rkv, u_lora, u_rkv, u_rkv, u_rkv, u_lora, *consts, hf0, hb0)


def _dft_tables(n):
    ang = 2.0 * np.pi * (np.outer(np.arange(n), np.arange(n)) % n) / n
    return np.cos(ang), np.sin(ang)


def _table(x):
    return jnp.asarray(x, F32).astype(BF16)


def _four_ctx_kernel(f_ref, wc_ref, ct_ref, st_ref, o_ref):
    y = _bf(_dot(_bf(f_ref[0]), wc_ref[...]))
    w = FOURIER_WIDTH
    o_ref[0] = _dot(ct_ref[...], y[:, :w]) - _dot(st_ref[...], y[:, w:])


def _fourier_ctx(f):
    b, t, w = f.shape
    gd = FOURIER_GROUP_DIM
    cc, sc = _dft_tables(gd)
    wc = np.zeros((w, 2 * w), np.float32)
    for g in range(FOURIER_GROUPS):
        wc[g * gd:(g + 1) * gd, g * gd:(g + 1) * gd] = cc
        wc[g * gd:(g + 1) * gd, w + g * gd:w + (g + 1) * gd] = sc
    ct, st = _dft_tables(t)
    norm = 1.0 / np.sqrt(t * gd)
    consts = [_table(wc), _table(ct * norm), _table(st * norm)]
    return pl.pallas_call(
        _four_ctx_kernel,
        grid=(b,),
        in_specs=[pl.BlockSpec((1, t, w), lambda bi: (bi, 0, 0))] + [_const_spec(x.shape) for x in consts],
        out_specs=pl.BlockSpec((1, t, w), lambda bi: (bi, 0, 0)),
        out_shape=jax.ShapeDtypeStruct(f.shape, F32),
        compiler_params=_params(("arbitrary",)),
        name="fourier_ctx",
    )(f, *consts)


W_ROWS = 4


def _four_lat_kernel(f_ref, wc_ref, kcw_ref, ksw_ref, kcr_ref, ksr_ref, o_ref, z_ref):
    gd = FOURIER_GROUP_DIM
    rows = f_ref.shape[1]
    x = _bf(f_ref[0].reshape(rows * GRID_W, gd))
    y = _bf(_dot(x, wc_ref[...]))
    kcw = kcw_ref[...]
    ksw = ksw_ref[...]
    blk = W_ROWS * GRID_W
    for j in range(rows // W_ROWS):
        yy = y[j * blk:(j + 1) * blk]
        pc = _dot(kcw, yy)
        ps = _dot(ksw, yy)
        z_ref[j * W_ROWS:(j + 1) * W_ROWS, :, :gd] = (pc[:, :gd] - ps[:, gd:]).reshape(W_ROWS, GRID_W, gd)
        z_ref[j * W_ROWS:(j + 1) * W_ROWS, :, gd:] = (ps[:, :gd] + pc[:, gd:]).reshape(W_ROWS, GRID_W, gd)
    kcr = kcr_ref[...]
    ksr = ksr_ref[...]
    for j in range(GRID_W // SUBLANES):
        ws = slice(j * SUBLANES, (j + 1) * SUBLANES)
        zg = _bf(z_ref[:, ws, :].reshape(rows * SUBLANES, 2 * gd))
        re = _dot(kcr, zg)[:, :gd] - _dot(ksr, zg)[:, gd:]
        o_ref[0, :, ws, :] = re.reshape(rows, SUBLANES, gd)


def _fourier_lat(f):
    b, t, w = f.shape
    gd = FOURIER_GROUP_DIM
    rows = t // GRID_W
    cc, sc = _dft_tables(gd)
    cw, sw = _dft_tables(GRID_W)
    cr, sr = _dft_tables(rows)
    norm = 1.0 / np.sqrt(rows * GRID_W * gd)
    eye_w = np.eye(W_ROWS)
    eye_8 = np.eye(SUBLANES)
    consts = [_table(np.concatenate([cc, sc], axis=1)),
              _table(np.kron(eye_w, cw)), _table(np.kron(eye_w, sw)),
              _table(np.kron(cr, eye_8) * norm), _table(np.kron(sr, eye_8) * norm)]
    f4 = f.reshape(b, rows, GRID_W, w)
    blk = pl.BlockSpec((1, rows, GRID_W, gd), lambda bi, g: (bi, 0, 0, g))
    out = pl.pallas_call(
        _four_lat_kernel,
        grid=(b, FOURIER_GROUPS),
        in_specs=[blk] + [_const_spec(x.shape) for x in consts],
        out_specs=blk,
        out_shape=jax.ShapeDtypeStruct(f4.shape, F32),
        scratch_shapes=[pltpu.VMEM((rows, GRID_W, 2 * gd), F32)],
        compiler_params=_params(("arbitrary", "arbitrary")),
        name="fourier_lat",
    )(f4, *consts)
    return out.reshape(b, t, w)


def _mixout_kernel(x_ref, mod_ref, of_ref, ob_ref, bonus_ref, lora_ref, four_ref, gate_ref,
                   eseg_ref, lnw_ref, lnb_ref, gup_ref, gb_ref, wo_ref, wf_ref, wout_ref, o_ref):
    eseg = eseg_ref[...]
    inv_n = 1.0 / HEAD_DIM
    o = of_ref[0] + ob_ref[0]
    mu = _head_sums(o, eseg) * inv_n
    dlt = o - mu
    var = _head_sums(dlt * dlt, eseg) * inv_n
    on = dlt * lax.rsqrt(var + GN_EPS) * lnw_ref[...] + lnb_ref[...]
    gd = lora_ref[0][:, N_DIRS * (DECAY_LORA + ICLR_LORA):]
    g = _dot(_bf(_sigmoid(gd)), gup_ref[...])
    y_rwkv = _dot(_bf((on + bonus_ref[0]) * g), wo_ref[...])
    y_four = _dot(_bf(four_ref[0]), wf_ref[...])
    gp = gate_ref[0]
    gb = gb_ref[...]
    g0 = _sigmoid(gp[:, :D_MODEL] + gb[0:1])
    g1 = _sigmoid(gp[:, D_MODEL:] + gb[1:2])
    merged = g0 * y_rwkv + g1 * y_four
    y = _dot(_bf(merged), wout_ref[...])
    o_ref[0] = x_ref[0] + mod_ref[0][5:6] * y


def _mixout(x, mod, per_batch_mod, o_f, o_b, bonus, u_lora, four, gate_pre, consts, *, tm):
    b, t, d = x.shape
    mod_map = (lambda bi, i: (bi + 1, 0, 0)) if per_batch_mod else (lambda bi, i: (0, 0, 0))

    def seq(w):
        return pl.BlockSpec((1, tm, w), lambda bi, i: (bi, i, 0))

    return pl.pallas_call(
        _mixout_kernel,
        grid=(b, t // tm),
        in_specs=([seq(d), pl.BlockSpec((1, N_MOD, d), mod_map), seq(RWKV_WIDTH), seq(RWKV_WIDTH),
                   seq(RWKV_WIDTH), seq(LORA_W), seq(FOURIER_WIDTH), seq(N_BRANCHES * D_MODEL)]
                  + [_const_spec(c.shape) for c in consts]),
        out_specs=seq(d),
        out_shape=jax.ShapeDtypeStruct(x.shape, F32),
        compiler_params=_params(("arbitrary", "arbitrary")),
        name="mixer_out",
    )(x, mod, o_f, o_b, bonus, u_lora, four, gate_pre, *consts)


def _state_to_pairs(s):
    b = s.shape[0]
    s = s.reshape(b, N_PAIRS, 2, HEAD_DIM, HEAD_DIM)
    return jnp.transpose(s, (0, 1, 4, 2, 3)).reshape(b, N_PAIRS, HEAD_DIM, PAIR)


def _pairs_to_state(hp):
    b = hp.shape[0]
    hp = hp.reshape(b, N_PAIRS, HEAD_DIM, 2, HEAD_DIM)
    return jnp.transpose(hp, (0, 1, 3, 4, 2)).reshape(b, RWKV_HEADS, HEAD_DIM, HEAD_DIM)


def _block_diag_lora(w_up):
    dirs, rank, width = w_up.shape
    out = jnp.zeros((dirs * rank, dirs * width), F32)
    for d in range(dirs):
        out = out.at[d * rank:(d + 1) * rank, d * width:(d + 1) * width].set(w_up[d])
    return _bf(out)


def kernel(x_prompt, x_sample, state_rwkv_fwd, state_rwkv_bwd, c, c_ctx, ada_w, ada_b, norm_w,
           ffn1_w13, ffn1_w2, ffn2_w13, ffn2_w2, w_in, gate_b, conv_rkv, w0, w_up, a0, a_up, g_up,
           k_k, k_a, r_k, lnx_w, lnx_b, w_o_rwkv, w_fourier, w_out, final_norm):
    depth = ada_w.shape[0]
    b_p = x_prompt.shape[0]
    b_s = x_sample.shape[0]
    rw = RWKV_WIDTH
    seg = np.arange(PAIR) // HEAD_DIM
    eseg = jnp.asarray(seg[:, None] == seg[None, :], BF16)
    cc = jnp.zeros((2 * SUBLANES, D_MODEL), F32).at[0].set(c_ctx).at[1:1 + b_s].set(c)
    zero_state = jnp.zeros((b_p, N_PAIRS, HEAD_DIM, PAIR), F32)

    x_p, x_s = x_prompt, x_sample
    new_f, new_b = [], []
    for l in range(depth):
        mod = _modulation(cc, ada_w[l], ada_b[l])[:1 + b_s].reshape(1 + b_s, N_MOD, D_MODEL)
        w13_1, w2_1 = _bf(ffn1_w13[l]), _bf(ffn1_w2[l])
        w13_2, w2_2 = _bf(ffn2_w13[l]), _bf(ffn2_w2[l])
        w_in_b = _bf(w_in[l])
        scan_consts = (conv_rkv[l], w0[l], _block_diag_lora(w_up[l]), a0[l], _block_diag_lora(a_up[l]),
                       k_k[l].reshape(1, rw), k_a[l].reshape(1, rw), r_k[l].reshape(1, rw), eseg)
        mix_consts = (eseg, lnx_w[l].reshape(1, rw), lnx_b[l].reshape(1, rw), _bf(g_up[l]), gate_b[l],
                      _bf(w_o_rwkv[l]), _bf(w_fourier[l]), _bf(w_out[l]))
        last = l == depth - 1

        def layer(x, per_batch_mod, hf0, hb0, grid, tm):
            x = _ffn(x, mod, per_batch_mod, norm_w[l], w13_1, w2_1, final_norm,
                     mod_off=0, norm_row=0, final=False, tm=tm)
            u_rkv, u_lora, f, gate_pre = _inproj(x, mod, per_batch_mod, norm_w[l], w_in_b, tm=tm)
            o_f, o_b, bonus, hf, hb = _scan(u_rkv, u_lora, hf0, hb0, *scan_consts)
            four = _fourier_lat(f) if grid else _fourier_ctx(f)
            x = _mixout(x, mod, per_batch_mod, o_f, o_b, bonus, u_lora, four, gate_pre, mix_consts, tm=tm)
            x = _ffn(x, mod, per_batch_mod, norm_w[l], w13_2, w2_2, final_norm,
                     mod_off=6, norm_row=2, final=last, tm=tm)
            return x, hf, hb

        x_p, hf, hb = layer(x_p, False, zero_state, zero_state, False, 256)
        new_f.append(_pairs_to_state(hf))
        new_b.append(_pairs_to_state(hb))
        hf0 = _state_to_pairs(state_rwkv_fwd[:, l])
        hb0 = _state_to_pairs(state_rwkv_bwd[:, l])
        x_s, _, _ = layer(x_s, True, hf0, hb0, True, 512)
    if depth == 0:
        raise ValueError("depth must be positive")
    return (x_p, x_s, jnp.stack(new_f, axis=1), jnp.stack(new_b, axis=1))
```

```python
import functools

import numpy as np
import jax
import jax.numpy as jnp
from jax import lax
from jax.experimental import pallas as pl
from jax.experimental.pallas import tpu as pltpu

D_MODEL = 1024
RWKV_HEADS = 8
HEAD_DIM = 64
RWKV_WIDTH = RWKV_HEADS * HEAD_DIM
DECAY_LORA = 64
ICLR_LORA = 64
GATE_LORA = 128
N_DIRS = 2
FOURIER_GROUPS = 4
FOURIER_GROUP_DIM = 128
FOURIER_WIDTH = FOURIER_GROUPS * FOURIER_GROUP_DIM
GRID_W = 64
N_BRANCHES = 2
D_FF = 2816
N_MOD = 9
NORM_EPS = 1e-6
GN_EPS = 64e-5

OFF_WD = 3 * RWKV_WIDTH
LORA_W = N_DIRS * DECAY_LORA + N_DIRS * ICLR_LORA + GATE_LORA
OFF_F = OFF_WD + LORA_W
OFF_GATE = OFF_F + FOURIER_WIDTH
PROJ_W = OFF_GATE + N_BRANCHES * D_MODEL

CHUNK = 64
PAIR = 2 * HEAD_DIM
N_PAIRS = RWKV_HEADS // 2
SUB_CHUNKS = 2
SUBLANES = 8
FF_CHUNK = 256
VMEM_LIMIT = 56 * 1024 * 1024

F32 = jnp.float32
BF16 = jnp.bfloat16


def _bf(x):
    return x.astype(BF16)


def _dot(a, b):
    return jnp.dot(a, b, preferred_element_type=F32)


def _split(x, parts):
    out = []
    rem = x
    for _ in range(parts):
        p = rem.astype(BF16)
        out.append(p)
        rem = rem - p.astype(F32)
    return out


def _head_sums(x, e_pair, parts=2):
    cols = []
    for j in range(x.shape[1] // PAIR):
        xs = x[:, j * PAIR:(j + 1) * PAIR]
        cols.append(sum(_dot(p, e_pair) for p in _split(xs, parts)))
    return jnp.concatenate(cols, axis=1)


def _sigmoid(x):
    return 1.0 / (1.0 + jnp.exp(-x))


def _rms(x, w):
    ms = jnp.mean(x * x, axis=-1, keepdims=True)
    return x * lax.rsqrt(ms + NORM_EPS) * w


def _const_spec(shape):
    nd = len(shape)
    return pl.BlockSpec(shape, lambda *_: (0,) * nd, pipeline_mode=pl.Buffered(1))


def _params(sem):
    return pltpu.CompilerParams(dimension_semantics=sem, vmem_limit_bytes=VMEM_LIMIT)


def _mod_kernel(c_ref, w_ref, b_ref, o_ref):
    c = c_ref[...]
    s = c * _sigmoid(c)
    o_ref[...] = _dot(_bf(s), _bf(w_ref[...])) + b_ref[...]


def _modulation(cc, ada_w, ada_b):
    rows, d = cc.shape
    n = ada_w.shape[1]
    tn = n // 8
    return pl.pallas_call(
        _mod_kernel,
        grid=(n // tn,),
        in_specs=[pl.BlockSpec((rows, d), lambda j: (0, 0)),
                  pl.BlockSpec((d, tn), lambda j: (0, j)),
                  pl.BlockSpec((1, tn), lambda j: (0, j))],
        out_specs=pl.BlockSpec((rows, tn), lambda j: (0, j)),
        out_shape=jax.ShapeDtypeStruct((rows, n), F32),
        compiler_params=_params(("arbitrary",)),
        name="adaln_mod",
    )(cc, ada_w, ada_b.reshape(1, n))


def _ffn_kernel(x_ref, mod_ref, nw_ref, w13_ref, w2_ref, fn_ref, o_ref, *, mod_off, norm_row, final):
    x = x_ref[0]
    m = mod_ref[0]
    shift = m[mod_off:mod_off + 1]
    scale = m[mod_off + 1:mod_off + 2]
    gate = m[mod_off + 2:mod_off + 3]
    h = _rms(x, nw_ref[norm_row:norm_row + 1]) * (1.0 + scale) + shift
    hb = _bf(h)
    acc = jnp.zeros(x.shape, F32)
    for j in range(D_FF // FF_CHUNK):
        a = _dot(hb, w13_ref[:, j * FF_CHUNK:(j + 1) * FF_CHUNK])
        b = _dot(hb, w13_ref[:, D_FF + j * FF_CHUNK:D_FF + (j + 1) * FF_CHUNK])
        g = (a * _sigmoid(a)) * b
        acc = acc + _dot(_bf(g), w2_ref[j * FF_CHUNK:(j + 1) * FF_CHUNK, :])
    y = x + 0.5 * gate * acc
    if final:
        y = _rms(y, fn_ref[...])
    o_ref[0] = y


def _ffn(x, mod, per_batch_mod, norm_w, w13, w2, final_norm, *, mod_off, norm_row, final, tm):
    b, t, d = x.shape
    mod_map = (lambda bi, i: (bi + 1, 0, 0)) if per_batch_mod else (lambda bi, i: (0, 0, 0))
    kern = functools.partial(_ffn_kernel, mod_off=mod_off, norm_row=norm_row, final=final)
    return pl.pallas_call(
        kern,
        grid=(b, t // tm),
        in_specs=[pl.BlockSpec((1, tm, d), lambda bi, i: (bi, i, 0)),
                  pl.BlockSpec((1, N_MOD, d), mod_map),
                  _const_spec(norm_w.shape),
                  _const_spec(w13.shape),
                  _const_spec(w2.shape),
                  _const_spec((1, d))],
        out_specs=pl.BlockSpec((1, tm, d), lambda bi, i: (bi, i, 0)),
        out_shape=jax.ShapeDtypeStruct(x.shape, F32),
        compiler_params=_params(("arbitrary", "arbitrary")),
        name="ffn_final" if final else "ffn",
    )(x, mod, norm_w, w13, w2, final_norm.reshape(1, d))


def _inproj_kernel(x_ref, mod_ref, nw_ref, w_ref, rkv_ref, lora_ref, f_ref, gate_ref):
    x = x_ref[0]
    m = mod_ref[0]
    h = _rms(x, nw_ref[1:2]) * (1.0 + m[4:5]) + m[3:4]
    hb = _bf(h)
    rkv_ref[0] = _dot(hb, w_ref[:, :OFF_WD])
    lora_ref[0] = _dot(hb, w_ref[:, OFF_WD:OFF_F])
    f_ref[0] = _dot(hb, w_ref[:, OFF_F:OFF_GATE])
    gate_ref[0] = _dot(hb, w_ref[:, OFF_GATE:])


def _inproj(x, mod, per_batch_mod, norm_w, w_in, *, tm):
    b, t, d = x.shape
    mod_map = (lambda bi, i: (bi + 1, 0, 0)) if per_batch_mod else (lambda bi, i: (0, 0, 0))
    widths = (OFF_WD, LORA_W, FOURIER_WIDTH, N_BRANCHES * D_MODEL)
    return pl.pallas_call(
        _inproj_kernel,
        grid=(b, t // tm),
        in_specs=[pl.BlockSpec((1, tm, d), lambda bi, i: (bi, i, 0)),
                  pl.BlockSpec((1, N_MOD, d), mod_map),
                  _const_spec(norm_w.shape),
                  _const_spec(w_in.shape)],
        out_specs=[pl.BlockSpec((1, tm, w), lambda bi, i: (bi, i, 0)) for w in widths],
        out_shape=[jax.ShapeDtypeStruct((b, t, w), F32) for w in widths],
        compiler_params=_params(("arbitrary", "arbitrary")),
        name="mixer_inproj",
    )(x, mod, norm_w, w_in)


def _bdot(a, b):
    return lax.dot_general(a, b, (((2,), (1,)), ((0,), (0,))), preferred_element_type=F32)


def _bdot_nt(a, b):
    return lax.dot_general(a, b, (((2,), (2,)), ((0,), (0,))), preferred_element_type=F32)


def _tri_inverse(ab, same_blk, eye_f, bd):
    c = ab.shape[1]
    n1 = jnp.where(same_blk[SUBLANES], ab, 0.0)
    n2 = _bdot(_bf(n1), bd(n1))
    n34 = _bdot(_bf(jnp.concatenate([n1, n2], axis=1)), bd(n2))
    t = eye_f - n1 + n2 - n34[:, :c]
    t = t + _bdot(_bf(t), bd(n34[:, c:]))
    k = SUBLANES
    while k < c:
        off = jnp.where(same_blk[2 * k] & jnp.logical_not(same_blk[k]), ab, 0.0)
        t = t - _bdot(_bf(t), bd(_bdot(_bf(off), bd(t))))
        k *= 2
    return t


def _pairs(x):
    return jnp.stack([x[k * CHUNK:(k + 1) * CHUNK, j * PAIR:(j + 1) * PAIR]
                      for k in range(x.shape[0] // CHUNK) for j in range(N_PAIRS)], axis=0)


def _block_operands(bi, n_blocks, reverse, d, rkv_ref, prev_ref, next_ref, lora_ref, prm, bonus_ref):
    c = CHUNK
    rw = RWKV_WIDTH
    u = rkv_ref[0]
    rows = u.shape[0]
    row = lax.broadcasted_iota(jnp.int32, (rows, 1), 0)
    prev_row = jnp.where(bi > 0, prev_ref[0][SUBLANES - 1:SUBLANES, :], 0.0)
    next_row = jnp.where(bi < n_blocks - 1, next_ref[0][0:1, :], 0.0)
    u_prev = jnp.where(row == 0, prev_row, pltpu.roll(u, 1, 0))
    u_next = jnp.where(row == rows - 1, next_row, pltpu.roll(u, rows - 1, 0))
    cw = prm["conv"][...]
    rkv = u_prev * cw[0:1] + u * cw[1:2] + u_next * cw[2:3]
    r = rkv[:, :rw]
    k = rkv[:, rw:2 * rw]
    v = rkv[:, 2 * rw:]

    lo = lora_ref[0]
    wd = _bf(jnp.tanh(lo[:, :N_DIRS * DECAY_LORA]))
    ad = _bf(lo[:, N_DIRS * DECAY_LORA:N_DIRS * (DECAY_LORA + ICLR_LORA)])
    dsl = slice(d * rw, (d + 1) * rw)
    zw = prm["w0"][d:d + 1] + _dot(wd, prm["wup"][:, dsl])
    w_log = jnp.minimum(zw, 0.0) - jnp.log(1.0 + jnp.exp(-jnp.abs(zw))) - 0.5
    lw = -jnp.exp(w_log)
    a = _sigmoid(prm["a0"][d:d + 1] + _dot(ad, prm["aup"][:, dsl]))
    k_a = prm["k_a"][...]
    eseg = prm["eseg"][...]
    kk = k * prm["k_k"][...]
    ss = _head_sums(kk * kk, eseg)
    kk = kk / jnp.maximum(jnp.sqrt(ss), 1e-12)
    kd = k * (1.0 + (a - 1.0) * k_a)
    b = kk * a

    if bonus_ref is not None:
        osl = slice((1 - d) * rw, (2 - d) * rw)
        a_o = _sigmoid(prm["a0"][1 - d:2 - d] + _dot(ad, prm["aup"][:, osl]))
        kd_o = k * (1.0 + (a_o - 1.0) * k_a)
        rk = _head_sums(r * prm["r_k"][...] * (kd + kd_o), eseg)
        bonus_ref[0] = rk * v

    ti = lax.broadcasted_iota(jnp.int32, (rows, rows), 0)
    si = lax.broadcasted_iota(jnp.int32, (rows, rows), 1)
    earlier = (si >= ti) if reverse else (si <= ti)
    tri = _bf((earlier & ((ti // c) == (si // c))).astype(F32))
    incl = sum(_dot(tri, p) for p in _split(lw, 3))
    last = 0 if reverse else c - 1
    ends = [incl[j * c + last:j * c + last + 1] for j in range(rows // c)]
    total = jnp.concatenate([jnp.broadcast_to(e, (c, rw)) for e in ends], axis=0)
    p_end = jnp.concatenate([jnp.broadcast_to(jnp.exp(e), (c, rw)) for e in ends], axis=0)
    e_ninc = jnp.exp(-incl)
    e_tot = jnp.exp(total - incl)
    return dict(kkm=_pairs(kk * jnp.exp(incl - lw)), rp=_pairs(r * jnp.exp(incl)),
                km=_pairs(kd * e_ninc), bm=_pairs(b * e_ninc), khat=_pairs(kd * e_tot),
                bhat=_pairs(b * e_tot), v=_pairs(v), p_end=_pairs(p_end))


def _scan_kernel(rkv_f, prev_f, next_f, lora_f, rkv_b, prev_b, next_b, lora_b,
                 conv_ref, w0_ref, wup_ref, a0_ref, aup_ref, kk_ref, ka_ref, rk_ref, eseg_ref,
                 hf0_ref, hb0_ref, of_ref, ob_ref, bonus_ref, hf_ref, hb_ref, *, n_blocks):
    i = pl.program_id(1)
    c = CHUNK
    n = HEAD_DIM
    npr = N_PAIRS
    nsub = SUB_CHUNKS
    half = nsub * npr
    p2 = N_DIRS * half

    @pl.when(i == 0)
    def _():
        hf_ref[...] = hf0_ref[...]
        hb_ref[...] = hb0_ref[...]

    prm = dict(conv=conv_ref, w0=w0_ref, wup=wup_ref, a0=a0_ref, aup=aup_ref, k_k=kk_ref,
               k_a=ka_ref, r_k=rk_ref, eseg=eseg_ref)
    opf = _block_operands(i, n_blocks, False, 0, rkv_f, prev_f, next_f, lora_f, prm, bonus_ref)
    opb = _block_operands(n_blocks - 1 - i, n_blocks, True, 1, rkv_b, prev_b, next_b, lora_b, prm, None)
    op = {key: jnp.concatenate([opf[key], opb[key]], axis=0) for key in opf}

    shape = (p2, c, PAIR)
    rev = lax.broadcasted_iota(jnp.int32, shape, 0) >= half
    ti = lax.broadcasted_iota(jnp.int32, shape, 1)
    lane = lax.broadcasted_iota(jnp.int32, shape, 2)
    lane_hi = lane >= n
    si = jnp.where(lane_hi, lane - n, lane)
    age = jnp.where(rev, ti - si, si - ti)
    strict_m = age < 0
    incl_m = age <= 0
    eye = age == 0
    same_blk = {}
    kb = SUBLANES
    while kb <= c:
        same_blk[kb] = (ti // kb) == (si // kb)
        kb *= 2

    lane_hi_step = lax.broadcasted_iota(jnp.int32, (N_DIRS * npr, c, PAIR), 2) >= n

    def bd(y):
        hi = lane_hi if y.shape[0] == p2 else lane_hi_step
        return _bf(jnp.concatenate([jnp.where(hi, 0.0, y), jnp.where(hi, y, 0.0)], axis=1))

    def diag_blocks(full):
        return jnp.where(lane_hi, full[:, n:], full[:, :n])

    kkm = op["kkm"]
    rp = op["rp"]
    x2 = _bf(jnp.concatenate([kkm, rp], axis=1))
    abk = _bdot_nt(x2, jnp.concatenate([bd(op["bm"]), bd(op["km"])], axis=1))
    ab2 = abk[:, :, :PAIR]
    ak2 = abk[:, :, PAIR:]
    a_b = jnp.where(strict_m, ab2[:, :c], 0.0)
    a_rb = _bf(jnp.where(incl_m, ab2[:, c:], 0.0))
    a_k = _bf(jnp.where(strict_m, ak2[:, :c], 0.0))
    a_rk = _bf(jnp.where(incl_m, ak2[:, c:], 0.0))
    tinv = _bf(_tri_inverse(a_b, same_blk, eye.astype(F32), bd))
    v_bd = bd(op["v"])
    av = _bdot(jnp.concatenate([a_k, a_rk], axis=1), v_bd)
    akv = av[:, :c]
    wu = _bdot(tinv, jnp.concatenate([bd(kkm), bd(akv)], axis=2))
    wt = wu[:, :, :PAIR]
    ut = wu[:, :, PAIR:]
    arb = _bdot(a_rb, jnp.concatenate([bd(wt), bd(ut)], axis=2))
    qt = _bf(rp - arb[:, :, :PAIR])
    ot = av[:, c:] - arb[:, :, PAIR:]
    bhat_t = _bf(jnp.swapaxes(op["bhat"], 1, 2))
    khat_t = _bf(jnp.swapaxes(op["khat"], 1, 2))
    bwu = _bdot(bhat_t, _bf(wu))
    mc = _bf(diag_blocks(bwu[:, :, :PAIR]))
    g = diag_blocks(_bdot(khat_t, _bf(op["v"])) - bwu[:, :, PAIR:])
    dec = jnp.where(eye, op["p_end"], 0.0)
    p_lo = jnp.sum(jnp.where(lane_hi, 0.0, dec), axis=2, keepdims=True)
    p_hi = jnp.sum(jnp.where(lane_hi, dec, 0.0), axis=2, keepdims=True)
    p_col = jnp.where(lane_hi, p_hi, p_lo)
    qm = jnp.concatenate([qt, mc], axis=1)

    hh = jnp.concatenate([hf_ref[0], hb_ref[0]], axis=0)
    for step in range(nsub):
        kf = step
        kr = nsub - 1 - step

        def ent(x, kf=kf, kr=kr):
            return jnp.concatenate([x[kf * npr:(kf + 1) * npr],
                                    x[half + kr * npr:half + (kr + 1) * npr]], axis=0)

        qmh = _bdot(ent(qm), bd(hh))
        o = qmh[:, :c] + ent(ot)
        hh = ent(p_col) * hh - qmh[:, c:] + ent(g)
        for j in range(npr):
            ps = slice(j * PAIR, (j + 1) * PAIR)
            of_ref[0, kf * c:(kf + 1) * c, ps] = o[j]
            ob_ref[0, kr * c:(kr + 1) * c, ps] = o[npr + j]
    hf_ref[0] = hh[:npr]
    hb_ref[0] = hh[npr:]


def _scan(u_rkv, u_lora, hf0, hb0, conv, w0, wup_bd, a0, aup_bd, k_k, k_a, r_k, eseg):
    b, t, _ = u_rkv.shape
    blk = SUB_CHUNKS * CHUNK
    n = t // blk
    per8 = blk // SUBLANES
    last8 = t // SUBLANES - 1
    rw = RWKV_WIDTH

    def fwd(bi, i):
        return (bi, i, 0)

    def fwd_prev(bi, i):
        return (bi, jnp.maximum(i * per8 - 1, 0), 0)

    def fwd_next(bi, i):
        return (bi, jnp.minimum((i + 1) * per8, last8), 0)

    def bwd(bi, i):
        return (bi, n - 1 - i, 0)

    def bwd_prev(bi, i):
        return (bi, jnp.maximum((n - 1 - i) * per8 - 1, 0), 0)

    def bwd_next(bi, i):
        return (bi, jnp.minimum((n - i) * per8, last8), 0)

    def seq_specs(main, prev, nxt):
        return [pl.BlockSpec((1, blk, 3 * rw), main),
                pl.BlockSpec((1, SUBLANES, 3 * rw), prev),
                pl.BlockSpec((1, SUBLANES, 3 * rw), nxt),
                pl.BlockSpec((1, blk, LORA_W), main)]

    state_spec = pl.BlockSpec((1, N_PAIRS, HEAD_DIM, PAIR), lambda bi, i: (bi, 0, 0, 0))
    consts = (conv, w0, wup_bd, a0, aup_bd, k_k, k_a, r_k, eseg)
    seq_shape = jax.ShapeDtypeStruct((b, t, rw), F32)
    return pl.pallas_call(
        functools.partial(_scan_kernel, n_blocks=n),
        grid=(b, n),
        in_specs=(seq_specs(fwd, fwd_prev, fwd_next) + seq_specs(bwd, bwd_prev, bwd_next)
                  + [_const_spec(x.shape) for x in consts] + [state_spec, state_spec]),
        out_specs=[pl.BlockSpec((1, blk, rw), fwd), pl.BlockSpec((1, blk, rw), bwd),
                   pl.BlockSpec((1, blk, rw), fwd), state_spec, state_spec],
        out_shape=[seq_shape, seq_shape, seq_shape,
                   jax.ShapeDtypeStruct(hf0.shape, F32), jax.ShapeDtypeStruct(hb0.shape, F32)],
        compiler_params=_params(("arbitrary", "arbitrary")),
        name="rwkv7_scan",
    )(u_rkv, u_rkv, u_rkv, u_lora, u_rkv, u_rkv, u_rkv, u_lora, *consts, hf0, hb0)


def _dft_tables(n):
    ang = 2.0 * np.pi * (np.outer(np.arange(n), np.arange(n)) % n) / n
    return np.cos(ang), np.sin(ang)


def _table(x):
    return jnp.asarray(x, F32).astype(BF16)


def _four_ctx_kernel(f_ref, wc_ref, ct_ref, st_ref, o_ref):
    y = _bf(_dot(_bf(f_ref[0]), wc_ref[...]))
    w = FOURIER_WIDTH
    o_ref[0] = _dot(ct_ref[...], y[:, :w]) - _dot(st_ref[...], y[:, w:])


def _fourier_ctx(f):
    b, t, w = f.shape
    gd = FOURIER_GROUP_DIM
    cc, sc = _dft_tables(gd)
    wc = np.zeros((w, 2 * w), np.float32)
    for g in range(FOURIER_GROUPS):
        wc[g * gd:(g + 1) * gd, g * gd:(g + 1) * gd] = cc
        wc[g * gd:(g + 1) * gd, w + g * gd:w + (g + 1) * gd] = sc
    ct, st = _dft_tables(t)
    norm = 1.0 / np.sqrt(t * gd)
    consts = [_table(wc), _table(ct * norm), _table(st * norm)]
    return pl.pallas_call(
        _four_ctx_kernel,
        grid=(b,),
        in_specs=[pl.BlockSpec((1, t, w), lambda bi: (bi, 0, 0))] + [_const_spec(x.shape) for x in consts],
        out_specs=pl.BlockSpec((1, t, w), lambda bi: (bi, 0, 0)),
        out_shape=jax.ShapeDtypeStruct(f.shape, F32),
        compiler_params=_params(("arbitrary",)),
        name="fourier_ctx",
    )(f, *consts)


W_ROWS = 4


def _four_lat_kernel(f_ref, wc_ref, kcw_ref, ksw_ref, kcr_ref, ksr_ref, o_ref, z_ref):
    gd = FOURIER_GROUP_DIM
    rows = f_ref.shape[1]
    x = _bf(f_ref[0].reshape(rows * GRID_W, gd))
    y = _bf(_dot(x, wc_ref[...]))
    kcw = kcw_ref[...]
    ksw = ksw_ref[...]
    blk = W_ROWS * GRID_W
    for j in range(rows // W_ROWS):
        yy = y[j * blk:(j + 1) * blk]
        pc = _dot(kcw, yy)
        ps = _dot(ksw, yy)
        z_ref[j * W_ROWS:(j + 1) * W_ROWS, :, :gd] = (pc[:, :gd] - ps[:, gd:]).reshape(W_ROWS, GRID_W, gd)
        z_ref[j * W_ROWS:(j + 1) * W_ROWS, :, gd:] = (ps[:, :gd] + pc[:, gd:]).reshape(W_ROWS, GRID_W, gd)
    kcr = kcr_ref[...]
    ksr = ksr_ref[...]
    for j in range(GRID_W // SUBLANES):
        ws = slice(j * SUBLANES, (j + 1) * SUBLANES)
        zg = _bf(z_ref[:, ws, :].reshape(rows * SUBLANES, 2 * gd))
        re = _dot(kcr, zg)[:, :gd] - _dot(ksr, zg)[:, gd:]
        o_ref[0, :, ws, :] = re.reshape(rows, SUBLANES, gd)


def _fourier_lat(f):
    b, t, w = f.shape
    gd = FOURIER_GROUP_DIM
    rows = t // GRID_W
    cc, sc = _dft_tables(gd)
    cw, sw = _dft_tables(GRID_W)
    cr, sr = _dft_tables(rows)
    norm = 1.0 / np.sqrt(rows * GRID_W * gd)
    eye_w = np.eye(W_ROWS)
    eye_8 = np.eye(SUBLANES)
    consts = [_table(np.concatenate([cc, sc], axis=1)),
              _table(np.kron(eye_w, cw)), _table(np.kron(eye_w, sw)),
              _table(np.kron(cr, eye_8) * norm), _table(np.kron(sr, eye_8) * norm)]
    f4 = f.reshape(b, rows, GRID_W, w)
    blk = pl.BlockSpec((1, rows, GRID_W, gd), lambda bi, g: (bi, 0, 0, g))
    out = pl.pallas_call(
        _four_lat_kernel,
        grid=(b, FOURIER_GROUPS),
        in_specs=[blk] + [_const_spec(x.shape) for x in consts],
        out_specs=blk,
        out_shape=jax.ShapeDtypeStruct(f4.shape, F32),
        scratch_shapes=[pltpu.VMEM((rows, GRID_W, 2 * gd), F32)],
        compiler_params=_params(("arbitrary", "arbitrary")),
        name="fourier_lat",
    )(f4, *consts)
    return out.reshape(b, t, w)


def _mixout_kernel(x_ref, mod_ref, of_ref, ob_ref, bonus_ref, lora_ref, four_ref, gate_ref,
                   eseg_ref, lnw_ref, lnb_ref, gup_ref, gb_ref, wo_ref, wf_ref, wout_ref, o_ref):
    eseg = eseg_ref[...]
    inv_n = 1.0 / HEAD_DIM
    o = of_ref[0] + ob_ref[0]
    mu = _head_sums(o, eseg) * inv_n
    dlt = o - mu
    var = _head_sums(dlt * dlt, eseg) * inv_n
    on = dlt * lax.rsqrt(var + GN_EPS) * lnw_ref[...] + lnb_ref[...]
    gd = lora_ref[0][:, N_DIRS * (DECAY_LORA + ICLR_LORA):]
    g = _dot(_bf(_sigmoid(gd)), gup_ref[...])
    y_rwkv = _dot(_bf((on + bonus_ref[0]) * g), wo_ref[...])
    y_four = _dot(_bf(four_ref[0]), wf_ref[...])
    gp = gate_ref[0]
    gb = gb_ref[...]
    g0 = _sigmoid(gp[:, :D_MODEL] + gb[0:1])
    g1 = _sigmoid(gp[:, D_MODEL:] + gb[1:2])
    merged = g0 * y_rwkv + g1 * y_four
    y = _dot(_bf(merged), wout_ref[...])
    o_ref[0] = x_ref[0] + mod_ref[0][5:6] * y


def _mixout(x, mod, per_batch_mod, o_f, o_b, bonus, u_lora, four, gate_pre, consts, *, tm):
    b, t, d = x.shape
    mod_map = (lambda bi, i: (bi + 1, 0, 0)) if per_batch_mod else (lambda bi, i: (0, 0, 0))

    def seq(w):
        return pl.BlockSpec((1, tm, w), lambda bi, i: (bi, i, 0))

    return pl.pallas_call(
        _mixout_kernel,
        grid=(b, t // tm),
        in_specs=([seq(d), pl.BlockSpec((1, N_MOD, d), mod_map), seq(RWKV_WIDTH), seq(RWKV_WIDTH),
                   seq(RWKV_WIDTH), seq(LORA_W), seq(FOURIER_WIDTH), seq(N_BRANCHES * D_MODEL)]
                  + [_const_spec(c.shape) for c in consts]),
        out_specs=seq(d),
        out_shape=jax.ShapeDtypeStruct(x.shape, F32),
        compiler_params=_params(("arbitrary", "arbitrary")),
        name="mixer_out",
    )(x, mod, o_f, o_b, bonus, u_lora, four, gate_pre, *consts)


def _state_to_pairs(s):
    b = s.shape[0]
    s = s.reshape(b, N_PAIRS, 2, HEAD_DIM, HEAD_DIM)
    return jnp.transpose(s, (0, 1, 4, 2, 3)).reshape(b, N_PAIRS, HEAD_DIM, PAIR)


def _pairs_to_state(hp):
    b = hp.shape[0]
    hp = hp.reshape(b, N_PAIRS, HEAD_DIM, 2, HEAD_DIM)
    return jnp.transpose(hp, (0, 1, 3, 4, 2)).reshape(b, RWKV_HEADS, HEAD_DIM, HEAD_DIM)


def _block_diag_lora(w_up):
    dirs, rank, width = w_up.shape
    out = jnp.zeros((dirs * rank, dirs * width), F32)
    for d in range(dirs):
        out = out.at[d * rank:(d + 1) * rank, d * width:(d + 1) * width].set(w_up[d])
    return _bf(out)


def kernel(x_prompt, x_sample, state_rwkv_fwd, state_rwkv_bwd, c, c_ctx, ada_w, ada_b, norm_w,
           ffn1_w13, ffn1_w2, ffn2_w13, ffn2_w2, w_in, gate_b, conv_rkv, w0, w_up, a0, a_up, g_up,
           k_k, k_a, r_k, lnx_w, lnx_b, w_o_rwkv, w_fourier, w_out, final_norm):
    depth = ada_w.shape[0]
    b_p = x_prompt.shape[0]
    b_s = x_sample.shape[0]
    rw = RWKV_WIDTH
    seg = np.arange(PAIR) // HEAD_DIM
    eseg = jnp.asarray(seg[:, None] == seg[None, :], BF16)
    cc = jnp.zeros((2 * SUBLANES, D_MODEL), F32).at[0].set(c_ctx).at[1:1 + b_s].set(c)
    zero_state = jnp.zeros((b_p, N_PAIRS, HEAD_DIM, PAIR), F32)

    x_p, x_s = x_prompt, x_sample
    new_f, new_b = [], []
    for l in range(depth):
        mod = _modulation(cc, ada_w[l], ada_b[l])[:1 + b_s].reshape(1 + b_s, N_MOD, D_MODEL)
        w13_1, w2_1 = _bf(ffn1_w13[l]), _bf(ffn1_w2[l])
        w13_2, w2_2 = _bf(ffn2_w13[l]), _bf(ffn2_w2[l])
        w_in_b = _bf(w_in[l])
        scan_consts = (conv_rkv[l], w0[l], _block_diag_lora(w_up[l]), a0[l], _block_diag_lora(a_up[l]),
                       k_k[l].reshape(1, rw), k_a[l].reshape(1, rw), r_k[l].reshape(1, rw), eseg)
        mix_consts = (eseg, lnx_w[l].reshape(1, rw), lnx_b[l].reshape(1, rw), _bf(g_up[l]), gate_b[l],
                      _bf(w_o_rwkv[l]), _bf(w_fourier[l]), _bf(w_out[l]))
        last = l == depth - 1

        def layer(x, per_batch_mod, hf0, hb0, grid, tm):
            x = _ffn(x, mod, per_batch_mod, norm_w[l], w13_1, w2_1, final_norm,
                     mod_off=0, norm_row=0, final=False, tm=tm)
            u_rkv, u_lora, f, gate_pre = _inproj(x, mod, per_batch_mod, norm_w[l], w_in_b, tm=tm)
            o_f, o_b, bonus, hf, hb = _scan(u_rkv, u_lora, hf0, hb0, *scan_consts)
            four = _fourier_lat(f) if grid else _fourier_ctx(f)
            x = _mixout(x, mod, per_batch_mod, o_f, o_b, bonus, u_lora, four, gate_pre, mix_consts, tm=tm)
            x = _ffn(x, mod, per_batch_mod, norm_w[l], w13_2, w2_2, final_norm,
                     mod_off=6, norm_row=2, final=last, tm=tm)
            return x, hf, hb

        x_p, hf, hb = layer(x_p, False, zero_state, zero_state, False, 256)
        new_f.append(_pairs_to_state(hf))
        new_b.append(_pairs_to_state(hb))
        hf0 = _state_to_pairs(state_rwkv_fwd[:, l])
        hb0 = _state_to_pairs(state_rwkv_bwd[:, l])
        x_s, _, _ = layer(x_s, True, hf0, hb0, True, 512)
    if depth == 0:
        raise ValueError("depth must be positive")
    return (x_p, x_s, jnp.stack(new_f, axis=1), jnp.stack(new_b, axis=1))
```
